```python
import jax, jax.numpy as jnp
from jax import lax
import numpy as np

D_MODEL = 2048
BATCH = 4
SEQ = 4096
DEPTH = 2

GRID_W = 64
CTX_LEN = 256
ROPE_THETA = 10000.0
NORM_EPS = 1e-6
Q_BLOCK = 128

GLA_HEADS = 4
GLA_DK = 64
GLA_DV = 128
GLA_GATE_RANK = 16
GLA_TAU = 16.0
GLA_CHUNK = 64

MLA_HEADS = 6
MLA_Q_RANK = 384
MLA_KV_RANK = 512
MLA_NOPE = 128
MLA_ROPE = 64
MLA_DV = 128

GQA_HEADS = 6
GQA_KV_HEADS = 2
GQA_DH = 128

MIX_WIDTH = GLA_HEADS * GLA_DV + MLA_HEADS * MLA_DV + GQA_HEADS * GQA_DH

IN_SIZES = (
    GLA_HEADS * GLA_DK,
    GLA_HEADS * GLA_DK,
    GLA_HEADS * GLA_DV,
    GLA_HEADS * GLA_DV,
    GLA_GATE_RANK,
    GLA_GATE_RANK,
    MLA_Q_RANK,
    MLA_KV_RANK,
    MLA_ROPE,
    GQA_HEADS * GQA_DH,
    GQA_KV_HEADS * GQA_DH,
    GQA_KV_HEADS * GQA_DH,
)
IN_WIDTH = sum(IN_SIZES)

N_EXPERTS = 64
TOP_K = 8
N_GROUPS = 8
TOPK_GROUPS = 4
D_EXPERT = 512
D_SHARED = 512
ROUTED_SCALE = 2.5
MOE_BLOCK = 256

kernel_name = "hybrid_gla_mla_gqa_moe_dit_block"


def _rmsnorm(x, g):
    xf = x.astype(jnp.float32)
    y = xf * lax.rsqrt(jnp.mean(xf * xf, axis=-1, keepdims=True) + NORM_EPS)
    return (y * g.astype(jnp.float32)).astype(x.dtype)


def _grid_positions(n_tokens):
    rows_count = n_tokens // GRID_W
    rows = jnp.repeat(jnp.arange(rows_count, dtype=jnp.int32), GRID_W)
    cols = jnp.tile(jnp.arange(GRID_W, dtype=jnp.int32), rows_count)
    return rows, cols


def _axial_rope(x, rows, cols):
    d = x.shape[-1]
    half = d // 2
    inv = ROPE_THETA ** (-jnp.arange(0, half, 2, dtype=jnp.float32) / half)
    xf = x.astype(jnp.float32)

    def rot(xa, p):
        ang = p.astype(jnp.float32)[:, None] * inv[None, :]
        cos = jnp.cos(ang)[None, :, None, :]
        sin = jnp.sin(ang)[None, :, None, :]
        x1, x2 = jnp.split(xa, 2, axis=-1)
        return jnp.concatenate([x1 * cos - x2 * sin, x2 * cos + x1 * sin], axis=-1)

    out = jnp.concatenate([rot(xf[..., :half], rows), rot(xf[..., half:], cols)], axis=-1)
    return out.astype(x.dtype)


def _block_attention(q, k, v, scale):
    B, Sq, Hk, G, Dk = q.shape
    nb = Sq // Q_BLOCK
    qb = jnp.moveaxis(q.reshape(B, nb, Q_BLOCK, Hk, G, Dk), 1, 0)

    def attend(qi):
        s = jnp.einsum('bqhgd,bkhd->bhgqk', qi, k, preferred_element_type=jnp.float32) * scale
        p = jax.nn.softmax(s, axis=-1)
        return jnp.einsum('bhgqk,bkhd->bqhgd', p.astype(v.dtype), v)

    o = lax.map(attend, qb)
    return jnp.moveaxis(o, 0, 1).reshape(B, Sq, Hk, G, v.shape[-1])


def _gla_chunked(q, k, v, la, s0):
    B, H, T, _ = q.shape
    dv = v.shape[-1]
    nc = T // GLA_CHUNK
    shp = lambda t: t.reshape(B, H, nc, GLA_CHUNK, t.shape[-1])
    q, k, v, la = shp(q), shp(k), shp(v), shp(la)
    b = jnp.cumsum(la, axis=3)
    b_last = b[:, :, :, -1:, :]
    q_dec = q * jnp.exp(b)
    k_inv = k * jnp.exp(-b)
    k_end = k * jnp.exp(b_last - b)
    mask = jnp.tril(jnp.ones((GLA_CHUNK, GLA_CHUNK), dtype=bool))
    att = jnp.where(mask, jnp.einsum('bhcld,bhcmd->bhclm', q_dec, k_inv), 0.0)
    o_intra = jnp.einsum('bhclm,bhcmv->bhclv', att, v)
    ds = jnp.einsum('bhcld,bhclv->bhcdv', k_end, v)
    decay = jnp.exp(b_last[:, :, :, 0, :])

    def step(s, inp):
        dc, dsc = inp
        return dc[..., None] * s + dsc, s

    s_fin, s_prev = lax.scan(step, s0, (jnp.moveaxis(decay, 2, 0), jnp.moveaxis(ds, 2, 0)))
    o_inter = jnp.einsum('bhcld,bhcdv->bhclv', q_dec, jnp.moveaxis(s_prev, 0, 2))
    return (o_intra + o_inter).reshape(B, H, T, dv), s_fin


def _gla_dir(q, k, v, la, s0, reverse):
    if reverse:
        fl = lambda t: jnp.flip(t, axis=2)
        o, s = _gla_chunked(fl(q), fl(k), fl(v), fl(la), s0)
        return fl(o), s
    return _gla_chunked(q, k, v, la, s0)


def _gla_stream(gq, gk, gv, gzf, gzb, wg2_f, bg_f, wg2_b, bg_b):
    B, S, _ = gq.shape
    heads = lambda t, d: t.reshape(B, S, GLA_HEADS, d).transpose(0, 2, 1, 3).astype(jnp.float32)
    q = heads(gq, GLA_DK) * (GLA_DK ** -0.5)
    k = heads(gk, GLA_DK)
    v = heads(gv, GLA_DV)
    la_f = heads(jax.nn.log_sigmoid((gzf @ wg2_f + bg_f).astype(jnp.float32)) / GLA_TAU, GLA_DK)
    la_b = heads(jax.nn.log_sigmoid((gzb @ wg2_b + bg_b).astype(jnp.float32)) / GLA_TAU, GLA_DK)
    return q, k, v, la_f, la_b


def _gla_out(o, gr, g_out):
    B, H, S, dv = o.shape
    o = _rmsnorm(o.transpose(0, 2, 1, 3), g_out).reshape(B, S, H * dv)
    return (o * jax.nn.silu(gr.astype(jnp.float32))).astype(gr.dtype)


def _gla_mixer(lat, ctx, wg2_f, bg_f, wg2_b, bg_b, g_out):
    ql, kl, vl, lfl, lbl = _gla_stream(lat[0], lat[1], lat[2], lat[4], lat[5], wg2_f, bg_f, wg2_b, bg_b)
    qc, kc, vc, lfc, lbc = _gla_stream(ctx[0], ctx[1], ctx[2], ctx[4], ctx[5], wg2_f, bg_f, wg2_b, bg_b)
    B = ql.shape[0]
    s0 = jnp.zeros((B, GLA_HEADS, GLA_DK, GLA_DV), jnp.float32)
    oc_f, sc_f = _gla_dir(qc, kc, vc, lfc, s0, False)
    oc_b, sc_b = _gla_dir(qc, kc, vc, lbc, s0, True)
    ol_f, _ = _gla_dir(ql, kl, vl, lfl, sc_f, False)
    ol_b, _ = _gla_dir(ql, kl, vl, lbl, sc_b, True)
    return _gla_out(ol_f + ol_b, lat[3], g_out), _gla_out(oc_f + oc_b, ctx[3], g_out)


def _mla_stream(cq, ckv, kr, g_q, w_uq, g_kv, w_ukv, pos):
    B, S, _ = cq.shape
    q = (_rmsnorm(cq, g_q) @ w_uq).reshape(B, S, MLA_HEADS, MLA_NOPE + MLA_ROPE)
    kv = (_rmsnorm(ckv, g_kv) @ w_ukv).reshape(B, S, MLA_HEADS, MLA_NOPE + MLA_DV)
    q_nope, q_rope = q[..., :MLA_NOPE], q[..., MLA_NOPE:]
    k_nope, v = kv[..., :MLA_NOPE], kv[..., MLA_NOPE:]
    k_rope = kr.reshape(B, S, 1, MLA_ROPE)
    if pos is not None:
        q_rope = _axial_rope(q_rope, pos[0], pos[1])
        k_rope = _axial_rope(k_rope, pos[0], pos[1])
    q = jnp.concatenate([q_nope, q_rope], axis=-1)
    k = jnp.concatenate([k_nope, jnp.broadcast_to(k_rope, (B, S, MLA_HEADS, MLA_ROPE))], axis=-1)
    return q, k, v


def _mla_mixer(lat, ctx, g_q, w_uq, g_kv, w_ukv, pos, want_ctx):
    ql, kl, vl = _mla_stream(lat[0], lat[1], lat[2], g_q, w_uq, g_kv, w_ukv, pos)
    qc, kc, vc = _mla_stream(ctx[0], ctx[1], ctx[2], g_q, w_uq, g_kv, w_ukv, None)
    scale = (MLA_NOPE + MLA_ROPE) ** -0.5
    B, S = ql.shape[:2]
    k_all = jnp.concatenate([kc, kl], axis=1)
    v_all = jnp.concatenate([vc, vl], axis=1)
    out_l = _block_attention(ql[:, :, :, None, :], k_all, v_all, scale).reshape(B, S, MLA_HEADS * MLA_DV)
    out_c = None
    if want_ctx:
        out_c = _block_attention(qc[:, :, :, None, :], kc, vc, scale).reshape(B, qc.shape[1], MLA_HEADS * MLA_DV)
    return out_l, out_c


def _gqa_stream(aq, ak, av, g_q, g_k, pos):
    B, S, _ = aq.shape
    q = _rmsnorm(aq.reshape(B, S, GQA_HEADS, GQA_DH), g_q)
    k = _rmsnorm(ak.reshape(B, S, GQA_KV_HEADS, GQA_DH), g_k)
    v = av.reshape(B, S, GQA_KV_HEADS, GQA_DH)
    if pos is not None:
        q = _axial_rope(q, pos[0], pos[1])
        k = _axial_rope(k, pos[0], pos[1])
    return q.reshape(B, S, GQA_KV_HEADS, GQA_HEADS // GQA_KV_HEADS, GQA_DH), k, v


def _gqa_mixer(lat, ctx, g_q, g_k, pos, want_ctx):
    ql, kl, vl = _gqa_stream(lat[0], lat[1], lat[2], g_q, g_k, pos)
    qc, kc, vc = _gqa_stream(ctx[0], ctx[1], ctx[2], g_q, g_k, None)
    scale = GQA_DH ** -0.5
    B, S = ql.shape[:2]
    out_l = _block_attention(ql, jnp.concatenate([kc, kl], axis=1),
                             jnp.concatenate([vc, vl], axis=1), scale).reshape(B, S, GQA_HEADS * GQA_DH)
    out_c = None
    if want_ctx:
        out_c = _block_attention(qc, kc, vc, scale).reshape(B, qc.shape[1], GQA_HEADS * GQA_DH)
    return out_l, out_c


def _swiglu(x, w1, w3, w2):
    return (jax.nn.silu(x @ w1) * (x @ w3)) @ w2


def _routed_experts(x, eidx, wts, w1, w3, w2):
    N, D = x.shape
    E = w1.shape[0]
    A = N * TOP_K
    flat_e = eidx.reshape(-1)
    flat_t = jnp.repeat(jnp.arange(N, dtype=jnp.int32), TOP_K)
    flat_w = wts.reshape(-1)
    order = jnp.argsort(flat_e)
    se = flat_e[order]
    counts = jnp.bincount(flat_e, length=E)
    padded = ((counts + MOE_BLOCK - 1) // MOE_BLOCK) * MOE_BLOCK
    pend = jnp.cumsum(padded)
    pstart = pend - padded
    sstart = jnp.cumsum(counts) - counts
    dest = pstart[se] + (jnp.arange(A, dtype=jnp.int32) - sstart[se])
    n_blocks = -(-A // MOE_BLOCK) + E
    P = n_blocks * MOE_BLOCK
    buf_t = jnp.full((P,), N, jnp.int32).at[dest].set(flat_t[order])
    buf_w = jnp.zeros((P,), jnp.float32).at[dest].set(flat_w[order])
    block_e = jnp.minimum(jnp.searchsorted(pend, jnp.arange(n_blocks) * MOE_BLOCK, side='right'), E - 1)
    x_pad = jnp.concatenate([x, jnp.zeros((1, D), x.dtype)], axis=0)

    def step(acc, blk):
        t, wb, e = blk
        xb = x_pad[t]
        yb = _swiglu(xb, w1[e], w3[e], w2[e]).astype(jnp.float32) * wb[:, None]
        return acc.at[t].add(yb), None

    acc0 = jnp.zeros((N + 1, D), jnp.float32)
    acc, _ = lax.scan(step, acc0, (buf_t.reshape(n_blocks, MOE_BLOCK),
                                   buf_w.reshape(n_blocks, MOE_BLOCK), block_e))
    return acc[:N].astype(x.dtype)


def _moe(x, router_w, router_b, w1, w3, w2, s1, s3, s2):
    N = x.shape[0]
    scores = jax.nn.sigmoid((x @ router_w).astype(jnp.float32))
    sel = scores + router_b.astype(jnp.float32)
    grp = sel.reshape(N, N_GROUPS, N_EXPERTS // N_GROUPS)
    grp_score = lax.top_k(grp, 2)[0].sum(-1)
    _, gidx = lax.top_k(grp_score, TOPK_GROUPS)
    gmask = jax.nn.one_hot(gidx, N_GROUPS).sum(axis=1) > 0
    emask = jnp.repeat(gmask, N_EXPERTS // N_GROUPS, axis=1)
    _, eidx = lax.top_k(jnp.where(emask, sel, -jnp.inf), TOP_K)
    w = jnp.take_along_axis(scores, eidx, axis=1)
    w = w / jnp.sum(w, axis=-1, keepdims=True) * ROUTED_SCALE
    return _routed_experts(x, eidx, w, w1, w3, w2) + _swiglu(x, s1, s3, s2)


def _modulation(cvec, w_mod, b_mod):
    return jnp.split(jax.nn.silu(cvec) @ w_mod + b_mod, 6, axis=-1)


def _adaln(x, g, shift, scale):
    return _rmsnorm(x, g) * (1.0 + scale) + shift


def _layer(x, xc, c, c_ctx, pos, want_ctx, w_mod, b_mod, g_pre_mix, g_post_mix, g_pre_ffn,
           g_post_ffn, w_in, gla_wg2_f, gla_bg_f, gla_wg2_b, gla_bg_b, gla_g_out, mla_g_q,
           mla_w_uq, mla_g_kv, mla_w_ukv, gqa_g_q, gqa_g_k, w_out, router_w, router_b,
           exp_w1, exp_w3, exp_w2, sh_w1, sh_w3, sh_w2):
    B, S, D = x.shape
    sh_m, sc_m, gt_m, sh_f, sc_f, gt_f = [m[:, None, :] for m in _modulation(c, w_mod, b_mod)]
    csh_m, csc_m, cgt_m, csh_f, csc_f, cgt_f = _modulation(c_ctx, w_mod, b_mod)
    offs = np.cumsum(IN_SIZES)[:-1]

    pl = jnp.split(_adaln(x, g_pre_mix, sh_m, sc_m) @ w_in, offs, axis=-1)
    pc = jnp.split(_adaln(xc, g_pre_mix, csh_m, csc_m) @ w_in, offs, axis=-1)
    gla_l, gla_c = _gla_mixer(pl[0:6], pc[0:6], gla_wg2_f, gla_bg_f, gla_wg2_b, gla_bg_b, gla_g_out)
    mla_l, mla_c = _mla_mixer(pl[6:9], pc[6:9], mla_g_q, mla_w_uq, mla_g_kv, mla_w_ukv, pos, want_ctx)
    gqa_l, gqa_c = _gqa_mixer(pl[9:12], pc[9:12], gqa_g_q, gqa_g_k, pos, want_ctx)
    y = jnp.concatenate([gla_l, mla_l, gqa_l], axis=-1) @ w_out
    x = x + gt_m * _rmsnorm(y, g_post_mix)

    if want_ctx:
        yc = jnp.concatenate([gla_c, mla_c, gqa_c], axis=-1) @ w_out
        xc = xc + cgt_m * _rmsnorm(yc, g_post_mix)
        hf = _adaln(x, g_pre_ffn, sh_f, sc_f).reshape(-1, D)
        hfc = _adaln(xc, g_pre_ffn, csh_f, csc_f).reshape(-1, D)
        yf = _moe(jnp.concatenate([hf, hfc], axis=0), router_w, router_b,
                  exp_w1, exp_w3, exp_w2, sh_w1, sh_w3, sh_w2)
        x = x + gt_f * _rmsnorm(yf[:B * S].reshape(B, S, D), g_post_ffn)
        xc = xc + cgt_f * _rmsnorm(yf[B * S:].reshape(xc.shape), g_post_ffn)
    else:
        hf = _adaln(x, g_pre_ffn, sh_f, sc_f).reshape(-1, D)
        yf = _moe(hf, router_w, router_b, exp_w1, exp_w3, exp_w2, sh_w1, sh_w3, sh_w2)
        x = x + gt_f * _rmsnorm(yf.reshape(B, S, D), g_post_ffn)
    return x, xc


def setup_inputs(seed: int = 0) -> dict:
    key = jax.random.key(seed)
    ks = iter(jax.random.split(key, 40))
    L, D = DEPTH, D_MODEL

    def nrm(shape, scale):
        return jax.random.normal(next(ks), shape, jnp.float32) * scale

    def gain(n):
        return 1.0 + nrm((L, n), 0.02)

    return {
        "x": nrm((BATCH, SEQ, D), 1.0),
        "c": nrm((BATCH, D), 1.0),
        "ctx": nrm((BATCH, CTX_LEN, D), 1.0),
        "c_ctx": nrm((D,), 1.0),
        "w_mod": nrm((L, D, 6 * D), 0.5 * D ** -0.5),
        "b_mod": nrm((L, 6 * D), 0.02),
        "g_pre_mix": gain(D),
        "g_post_mix": gain(D),
        "g_pre_ffn": gain(D),
        "g_post_ffn": gain(D),
        "w_in": nrm((L, D, IN_WIDTH), D ** -0.5),
        "gla_wg2_f": nrm((L, GLA_GATE_RANK, GLA_HEADS * GLA_DK), GLA_GATE_RANK ** -0.5),
        "gla_bg_f": nrm((L, GLA_HEADS * GLA_DK), 0.1),
        "gla_wg2_b": nrm((L, GLA_GATE_RANK, GLA_HEADS * GLA_DK), GLA_GATE_RANK ** -0.5),
        "gla_bg_b": nrm((L, GLA_HEADS * GLA_DK), 0.1),
        "gla_g_out": gain(GLA_DV),
        "mla_g_q": gain(MLA_Q_RANK),
        "mla_w_uq": nrm((L, MLA_Q_RANK, MLA_HEADS * (MLA_NOPE + MLA_ROPE)), MLA_Q_RANK ** -0.5),
        "mla_g_kv": gain(MLA_KV_RANK),
        "mla_w_ukv": nrm((L, MLA_KV_RANK, MLA_HEADS * (MLA_NOPE + MLA_DV)), MLA_KV_RANK ** -0.5),
        "gqa_g_q": gain(GQA_DH),
        "gqa_g_k": gain(GQA_DH),
        "w_out": nrm((L, MIX_WIDTH, D), MIX_WIDTH ** -0.5),
        "router_w": nrm((L, D, N_EXPERTS), D ** -0.5),
        "router_b": nrm((L, N_EXPERTS), 0.01),
        "exp_w1": nrm((L, N_EXPERTS, D, D_EXPERT), D ** -0.5),
        "exp_w3": nrm((L, N_EXPERTS, D, D_EXPERT), D ** -0.5),
        "exp_w2": nrm((L, N_EXPERTS, D_EXPERT, D), D_EXPERT ** -0.5),
        "sh_w1": nrm((L, D, D_SHARED), D ** -0.5),
        "sh_w3": nrm((L, D, D_SHARED), D ** -0.5),
        "sh_w2": nrm((L, D_SHARED, D), D_SHARED ** -0.5),
    }


def reference(x, c, ctx, c_ctx, w_mod, b_mod, g_pre_mix, g_post_mix, g_pre_ffn, g_post_ffn,
              w_in, gla_wg2_f, gla_bg_f, gla_wg2_b, gla_bg_b, gla_g_out, mla_g_q, mla_w_uq,
              mla_g_kv, mla_w_ukv, gqa_g_q, gqa_g_k, w_out, router_w, router_b,
              exp_w1, exp_w3, exp_w2, sh_w1, sh_w3, sh_w2):
    pos = _grid_positions(x.shape[1])
    xc = ctx
    for i in range(DEPTH):
        x, xc = _layer(x, xc, c, c_ctx, pos, i < DEPTH - 1,
                       w_mod[i], b_mod[i], g_pre_mix[i], g_post_mix[i], g_pre_ffn[i],
                       g_post_ffn[i], w_in[i], gla_wg2_f[i], gla_bg_f[i], gla_wg2_b[i],
                       gla_bg_b[i], gla_g_out[i], mla_g_q[i], mla_w_uq[i], mla_g_kv[i],
                       mla_w_ukv[i], gqa_g_q[i], gqa_g_k[i], w_out[i], router_w[i], router_b[i],
                       exp_w1[i], exp_w3[i], exp_w2[i], sh_w1[i], sh_w3[i], sh_w2[i])
    return x
```

```python
import functools
import math

import numpy as np
import jax
import jax.numpy as jnp
from jax import lax
from jax.experimental import pallas as pl
from jax.experimental.pallas import tpu as pltpu

F32 = jnp.float32
BF = jnp.bfloat16
U32 = jnp.uint32
I32 = jnp.int32

D = 2048
GRID_W = 64
CT = 256
ROPE_THETA = 10000.0
EPS = 1e-6
GLA_H, GLA_DK, GLA_DV, GLA_RANK, GLA_TAU, GLA_CHUNK = 4, 64, 128, 16, 16.0, 64
MLA_H, MLA_QR, MLA_KVR, MLA_NOPE, MLA_ROPE, MLA_DV = 6, 384, 512, 128, 64, 128
GQA_H, GQA_KV, GQA_DH = 6, 2, 128
N_EXP, TOP_K, N_GRP, TOPK_GRP, D_EXP, D_SH = 64, 8, 8, 4, 512, 512
ROUTED_SCALE = 2.5
LOG2E = math.log2(math.e)
HALF = D // 2

OFF_GV, OFF_GR, OFF_CKV, OFF_AQ, OFF_CQ, OFF_KRZ, OFF_GQ, OFF_GK, OFF_AK, OFF_AV = (
    0, 512, 1024, 1536, 2304, 2688, 2816, 3072, 3328, 3584)
PROJ_W = 3840
VMEM_LIMIT = 56 * 1024 * 1024


def _cp(sem, vmem=None):
    return pltpu.CompilerParams(dimension_semantics=sem, vmem_limit_bytes=vmem or VMEM_LIMIT)


def _dot(a, b):
    return jnp.dot(a, b, preferred_element_type=F32)


def _dot_nt(a, b):
    return lax.dot_general(a, b, (((1,), (1,)), ((), ())), preferred_element_type=F32)


def _dot_tn(a, b):
    return lax.dot_general(a, b, (((0,), (0,)), ((), ())), preferred_element_type=F32)


def _sigmoid(x):
    return 1.0 / (1.0 + jnp.exp(-x))


def _rms(x, g):
    return x * lax.rsqrt(jnp.mean(x * x, axis=-1, keepdims=True) + EPS) * g


def _split2(x):
    hi = x.astype(BF)
    return hi, (x - hi.astype(F32)).astype(BF)


def _pack_halves(y):
    bits = lax.bitcast_convert_type(y.astype(BF).astype(F32), U32)
    return (bits[:, :HALF] >> 16) | (bits[:, HALF:] & jnp.uint32(0xFFFF0000))


def _unpack_halves(w):
    lo = lax.bitcast_convert_type(w << 16, F32)
    hi = lax.bitcast_convert_type(w & jnp.uint32(0xFFFF0000), F32)
    return lo, hi


def _mod_kernel(a_ref, w_ref, b_ref, o_ref):
    a = a_ref[...]
    a = a * _sigmoid(a)
    ah, al = _split2(a)
    wh, wl = _split2(w_ref[0])
    o_ref[0] = _dot(ah, wh) + _dot(al, wh) + _dot(ah, wl) + b_ref[0]


def _modulation(cvecs, w_mod, b_mod):
    L, _, n = w_mod.shape
    tn = 1024
    return pl.pallas_call(
        _mod_kernel,
        grid=(L, n // tn),
        in_specs=[pl.BlockSpec((8, D), lambda l, j: (0, 0)),
                  pl.BlockSpec((1, D, tn), lambda l, j: (l, 0, j)),
                  pl.BlockSpec((1, 1, tn), lambda l, j: (l, 0, j))],
        out_specs=pl.BlockSpec((1, 8, tn), lambda l, j: (l, 0, j)),
        out_shape=jax.ShapeDtypeStruct((L, 8, n), F32),
        compiler_params=_cp(("arbitrary", "arbitrary")),
        name="modulation",
    )(cvecs, w_mod, b_mod.reshape(L, 1, n))


def _inproj_kernel(x_ref, g_ref, sh_ref, sc_ref, w_ref, o_ref, a_scr):
    @pl.when(pl.program_id(1) == 0)
    def _():
        y = _rms(x_ref[...], g_ref[...])
        a_scr[...] = (y * (1.0 + sc_ref[0]) + sh_ref[0]).astype(BF)

    o_ref[...] = _dot(a_scr[...], w_ref[...]).astype(o_ref.dtype)


def _inproj(xr, g, sh, sc, w, *, tm, tiles_per_batch, nb):
    R = xr.shape[0]
    tn = 768
    mod_idx = lambda i, j: (jnp.minimum(i // tiles_per_batch, nb), 0, 0)
    return pl.pallas_call(
        _inproj_kernel,
        grid=(R // tm, PROJ_W // tn),
        in_specs=[pl.BlockSpec((tm, D), lambda i, j: (i, 0)),
                  pl.BlockSpec((1, D), lambda i, j: (0, 0)),
                  pl.BlockSpec((1, 1, D), mod_idx),
                  pl.BlockSpec((1, 1, D), mod_idx),
                  pl.BlockSpec((D, tn), lambda i, j: (0, j))],
        out_specs=pl.BlockSpec((tm, tn), lambda i, j: (i, j)),
        out_shape=jax.ShapeDtypeStruct((R, PROJ_W), BF),
        scratch_shapes=[pltpu.VMEM((tm, D), BF)],
        compiler_params=_cp(("arbitrary", "arbitrary")),
        name="inproj",
    )(xr, g, sh, sc, w)


PREP_TM = 256
QS_MLA = (MLA_NOPE + MLA_ROPE) ** -0.5 * LOG2E
QS_GQA = GQA_DH ** -0.5 * LOG2E


def _prep_kernel(cq_ref, ckv_ref, krz_ref, aq_ref, ak_ref, gq_ref, gkv_ref, wuq_ref, wk_ref, wv_ref,
                 gaq_ref, gak_ref, cg_ref, sg_ref, cm_ref, sm_ref,
                 qm_ref, km_ref, vm_ref, qg_ref, kg_ref):
    lane = lax.broadcasted_iota(I32, (PREP_TM, 128), 1)

    def rope(t, c, s, half):
        nxt = pltpu.roll(t, 128 - half, 1)
        prv = pltpu.roll(t, half, 1)
        return t * c + jnp.where((lane % (2 * half)) < half, nxt, prv) * s

    cm, sm = cm_ref[...], sm_ref[...]
    cg, sg = cg_ref[...], sg_ref[...]

    q = _dot(_rms(cq_ref[...].astype(F32), gq_ref[...]).astype(BF), wuq_ref[...])
    for h in range(MLA_H):
        a = 256 * h
        qm_ref[:, a:a + 128] = (q[:, a:a + 128] * QS_MLA).astype(BF)
        qm_ref[:, a + 128:a + 256] = (rope(q[:, a + 128:a + 256], cm, sm, MLA_ROPE // 4) * QS_MLA).astype(BF)

    n = _rms(ckv_ref[...].astype(F32), gkv_ref[...]).astype(BF)
    kn = _dot(n, wk_ref[...])
    vm_ref[...] = _dot(n, wv_ref[...]).astype(BF)
    kr = rope(krz_ref[...].astype(F32), cm, sm, MLA_ROPE // 4).astype(BF)
    for h in range(MLA_H):
        a = 256 * h
        km_ref[:, a:a + 128] = kn[:, 128 * h:128 * h + 128].astype(BF)
        km_ref[:, a + 128:a + 256] = kr

    for h in range(GQA_H):
        t = _rms(aq_ref[:, 128 * h:128 * h + 128].astype(F32), gaq_ref[...])
        qg_ref[:, 128 * h:128 * h + 128] = (rope(t, cg, sg, GQA_DH // 4) * QS_GQA).astype(BF)
    for h in range(GQA_KV):
        t = _rms(ak_ref[:, 128 * h:128 * h + 128].astype(F32), gak_ref[...])
        kg_ref[:, 128 * h:128 * h + 128] = rope(t, cg, sg, GQA_DH // 4).astype(BF)


def _rope_tables(S):
    pos = np.arange(S)
    rows, cols = pos // GRID_W, pos % GRID_W

    def tab(d, width):
        half = d // 2
        inv = ROPE_THETA ** (-np.arange(0, half, 2, dtype=np.float64) / half)
        ar, ac = rows[:, None] * inv[None], cols[:, None] * inv[None]
        c = np.concatenate([np.cos(ar), np.cos(ar), np.cos(ac), np.cos(ac)], 1)
        s = np.concatenate([-np.sin(ar), np.sin(ar), -np.sin(ac), np.sin(ac)], 1)
        ci = np.ones((CT, d)); si = np.zeros((CT, d))
        c = np.concatenate([c, ci], 0); s = np.concatenate([s, si], 0)
        pad = np.zeros((S + CT, width - d))
        return (jnp.asarray(np.concatenate([c, pad], 1), F32), jnp.asarray(np.concatenate([s, pad], 1), F32))

    return tab(GQA_DH, 128) + tab(MLA_ROPE, 128)


def _prep(proj, gq, gkv, wuq, wk, wv, gaq, gak, tabs, *, S, RL):
    R = proj.shape[0]
    tm = PREP_TM
    nlat = RL // tm
    spb = S // tm
    tab_idx = lambda i: (jnp.where(i < nlat, i % spb, spb), 0)
    col = lambda w, off: pl.BlockSpec((tm, w), lambda i: (i, off // w))
    full = lambda a: pl.BlockSpec(a.shape, lambda i: (0,) * a.ndim)
    tspec = pl.BlockSpec((tm, 128), tab_idx)
    outs = [(MLA_H * 256, BF), (MLA_H * 256, BF), (MLA_H * MLA_DV, BF), (GQA_H * GQA_DH, BF), (GQA_KV * GQA_DH, BF)]
    return pl.pallas_call(
        _prep_kernel,
        grid=(R // tm,),
        in_specs=[col(MLA_QR, OFF_CQ), col(MLA_KVR, OFF_CKV), col(128, OFF_KRZ), col(768, OFF_AQ), col(256, OFF_AK),
                  full(gq), full(gkv), full(wuq), full(wk), full(wv), full(gaq), full(gak),
                  tspec, tspec, tspec, tspec],
        out_specs=[pl.BlockSpec((tm, w), lambda i: (i, 0)) for w, _ in outs],
        out_shape=[jax.ShapeDtypeStruct((R, w), dt) for w, dt in outs],
        compiler_params=_cp(("arbitrary",)),
        name="qk_prep",
    )(proj, proj, proj, proj, proj, gq, gkv, wuq, wk, wv, gaq, gak, *tabs)


GLA_TM = 256


def _gla_kernel(q_ref, k_ref, v_ref, z_ref, wg_ref, bg_ref, tri_ref, o_ref, st_scr, la_scr):
    d = pl.program_id(1)

    @pl.when(pl.program_id(2) == 0)
    def _():
        st_scr[...] = jnp.zeros_like(st_scr)

    z = _dot(z_ref[...], wg_ref[0]) + bg_ref[0]
    la_scr[...] = (jnp.minimum(z, 0.0) - jnp.log(1.0 + jnp.exp(-jnp.abs(z)))) * (1.0 / GLA_TAU)
    tri = tri_ref[0]
    keep = tri > 0
    C = GLA_CHUNK
    for cc in range(GLA_TM // C):
        c = cc + d * (GLA_TM // C - 1 - 2 * cc)
        r0 = pl.multiple_of(c * C, C)
        la = la_scr[pl.ds(r0, C), :]
        h1 = la.astype(BF)
        r1 = la - h1.astype(F32)
        h2 = r1.astype(BF)
        h3 = (r1 - h2.astype(F32)).astype(BF)
        b = _dot(tri, h1) + _dot(tri, h2) + _dot(tri, h3)
        tot = jnp.sum(la, axis=0, keepdims=True)
        q = q_ref[pl.ds(r0, C), :].astype(F32)
        k = k_ref[pl.ds(r0, C), :].astype(F32)
        qd = (q * (GLA_DK ** -0.5) * jnp.exp(b)).astype(BF)
        ki = (k * jnp.exp(-b)).astype(BF)
        ke = (k * jnp.exp(tot - b)).astype(BF)
        dec = jnp.exp(tot)
        v = v_ref[pl.ds(r0, C), :]
        for h in range(GLA_H):
            sl = slice(GLA_DK * h, GLA_DK * (h + 1))
            sv = slice(GLA_DV * h, GLA_DV * (h + 1))
            att = jnp.where(keep, _dot_nt(qd[:, sl], ki[:, sl]), 0.0).astype(BF)
            st = st_scr[h]
            o_ref[pl.ds(r0, C), sv] = _dot(att, v[:, sv]) + _dot_nt(qd[:, sl], st.astype(BF))
            st_scr[h] = st * dec[:, sl] + _dot_tn(v[:, sv], ke[:, sl])


def _gla(proj, wg, bg, tri, *, B, S, RL):
    R = proj.shape[0]
    tm = GLA_TM
    nj = S // tm + 1
    lat0 = lambda b: b * (S // tm)

    def rb(b, d, j):
        jl = jnp.where(d == 0, j - 1, S // tm - j)
        return jnp.where(j == 0, RL // tm + b, lat0(b) + jl)

    col = lambda w, off: pl.BlockSpec((tm, w), lambda b, d, j: (rb(b, d, j), off // w))
    sel = lambda a: pl.BlockSpec((1,) + a.shape[1:], lambda b, d, j: (d,) + (0,) * (a.ndim - 1))
    return pl.pallas_call(
        _gla_kernel,
        grid=(B, 2, nj),
        in_specs=[col(256, OFF_GQ), col(256, OFF_GK), col(512, OFF_GV), col(128, OFF_KRZ), sel(wg), sel(bg), sel(tri)],
        out_specs=pl.BlockSpec((None, tm, GLA_H * GLA_DV), lambda b, d, j: (d, rb(b, d, j), 0)),
        out_shape=jax.ShapeDtypeStruct((2, R, GLA_H * GLA_DV), F32),
        scratch_shapes=[pltpu.VMEM((GLA_H, GLA_DV, GLA_DK), F32), pltpu.VMEM((tm, GLA_H * GLA_DK), F32)],
        compiler_params=_cp(("arbitrary", "arbitrary", "arbitrary")),
        name="gla",
    )(proj, proj, proj, proj, wg, bg, tri)


def _flash_kernel(*refs, nseg, seg_lens, tq, dv, ck):
    q_ref = refs[0]
    k_refs = refs[1:1 + nseg]
    v_refs = refs[1 + nseg:1 + 2 * nseg]
    o_ref = refs[1 + 2 * nseg]
    s_scr = refs[2 + 2 * nseg]
    q = q_ref[...]
    m = jnp.full((tq, 128), -jnp.inf, F32)
    chunks = []
    off = 0
    for si in range(nseg):
        for c0 in range(0, seg_lens[si], ck):
            cl = min(ck, seg_lens[si] - c0)
            s = _dot_nt(q, k_refs[si][c0:c0 + cl, :])
            s_scr[:, off:off + cl] = s
            for l0 in range(0, cl, 128):
                m = jnp.maximum(m, s[:, l0:l0 + 128])
            chunks.append((si, c0, cl, off))
            off += cl
    mrow = jnp.max(m, axis=1, keepdims=True)
    l = jnp.zeros((tq, 128), F32)
    acc = jnp.zeros((tq, dv), F32)
    for si, c0, cl, off in chunks:
        p = jnp.exp2(s_scr[:, off:off + cl] - mrow)
        for l0 in range(0, cl, 128):
            l = l + p[:, l0:l0 + 128]
        acc = acc + _dot(p.astype(BF), v_refs[si][c0:c0 + cl, :])
    o_ref[...] = (acc / jnp.sum(l, axis=1, keepdims=True)).astype(o_ref.dtype)


def _attention(qa, ka, va, *, B, S, RL, hq, group, dk, dv, vbase, ctx_queries, name):
    cb = RL // CT
    kh = lambda h: h // group
    if ctx_queries:
        tq, nq, n_out = CT, 1, B * CT
        q_spec = pl.BlockSpec((tq, dk), lambda b, h, i: (cb + b, h))
        o_spec = pl.BlockSpec((tq, dv), lambda b, h, i: (b, h))
        k_specs = [pl.BlockSpec((CT, dk), lambda b, h, i: (cb + b, kh(h)))]
        v_specs = [pl.BlockSpec((CT, dv), lambda b, h, i: (cb + b, vbase + kh(h)))]
        seg_lens = (CT,)
    else:
        tq, nq, n_out = 256, S // 256, RL
        q_spec = pl.BlockSpec((tq, dk), lambda b, h, i: (b * nq + i, h))
        o_spec = pl.BlockSpec((tq, dv), lambda b, h, i: (b * nq + i, h))
        k_specs = [pl.BlockSpec((CT, dk), lambda b, h, i: (cb + b, kh(h))),
                   pl.BlockSpec((S, dk), lambda b, h, i: (b, kh(h)))]
        v_specs = [pl.BlockSpec((CT, dv), lambda b, h, i: (cb + b, vbase + kh(h))),
                   pl.BlockSpec((S, dv), lambda b, h, i: (b, vbase + kh(h)))]
        seg_lens = (CT, S)
    nseg = len(seg_lens)
    kern = functools.partial(_flash_kernel, nseg=nseg, seg_lens=seg_lens, tq=tq, dv=dv, ck=512)
    return pl.pallas_call(
        kern,
        grid=(B, hq, nq),
        in_specs=[q_spec] + k_specs + v_specs,
        out_specs=o_spec,
        out_shape=jax.ShapeDtypeStruct((n_out, hq * dv), BF),
        scratch_shapes=[pltpu.VMEM((tq, sum(seg_lens)), F32)],
        compiler_params=_cp(("arbitrary", "arbitrary", "arbitrary")),
        name=name,
    )(qa, *([ka] * nseg), *([va] * nseg))


WOUT_TM = 256


def _route(logit, rb):
    T = logit.shape[1]
    gsz = N_EXP // N_GRP
    scores = _sigmoid(logit)
    sel = scores + rb
    ninf = jnp.float32(-jnp.inf)
    sel3 = sel.reshape(N_GRP, gsz, T)
    idx3 = lax.broadcasted_iota(I32, (N_GRP, gsz, T), 1)
    m1 = jnp.max(sel3, axis=1, keepdims=True)
    first = jnp.min(jnp.where(sel3 == m1, idx3, gsz), axis=1, keepdims=True)
    m2 = jnp.max(jnp.where(idx3 == first, ninf, sel3), axis=1, keepdims=True)
    gs = (m1 + m2).reshape(N_GRP, T)
    gi = lax.broadcasted_iota(I32, (N_GRP, T), 0)
    grank = jnp.zeros((N_GRP, T), I32)
    for j in range(N_GRP):
        row = gs[j:j + 1, :]
        grank = grank + ((row > gs) | ((row == gs) & (j < gi))).astype(I32)
    gkeep = (grank < TOPK_GRP).reshape(N_GRP, 1, T)
    selm = jnp.where(jnp.broadcast_to(gkeep, (N_GRP, gsz, T)), sel3, ninf).reshape(N_EXP, T)
    ei = lax.broadcasted_iota(I32, (N_EXP, T), 0)
    erank = jnp.zeros((N_EXP, T), I32)
    for j in range(N_EXP):
        row = selm[j:j + 1, :]
        erank = erank + ((row > selm) | ((row == selm) & (j < ei))).astype(I32)
    chosen = erank < TOP_K
    w = jnp.where(chosen, scores, 0.0)
    wd = w / jnp.sum(w, axis=0, keepdims=True) * ROUTED_SCALE
    return wd, chosen.astype(I32)


def _wout_kernel(of_ref, ob_ref, gr_ref, mla_ref, gqa_ref, x_ref, w_ref, gout_ref, gpm_ref, gtm_ref,
                 gpf_ref, shf_ref, scf_ref, rwt_ref, rb_ref,
                 xn_ref, hfp_ref, wd_ref, msk_ref):
    og = of_ref[...] + ob_ref[...]
    gr = gr_ref[...].astype(F32)
    parts = []
    for h in range(GLA_H):
        sv = slice(GLA_DV * h, GLA_DV * (h + 1))
        g = gr[:, sv]
        parts.append((_rms(og[:, sv], gout_ref[...]) * (g * _sigmoid(g))).astype(BF))
    gla = jnp.concatenate(parts, axis=1)
    n_g, n_m = GLA_H * GLA_DV, MLA_H * MLA_DV
    y = (_dot(gla, w_ref[0:n_g, :]) + _dot(mla_ref[...], w_ref[n_g:n_g + n_m, :])
         + _dot(gqa_ref[...], w_ref[n_g + n_m:, :]))
    xn = x_ref[...] + gtm_ref[0] * _rms(y, gpm_ref[...])
    xn_ref[...] = xn
    hf = _rms(xn, gpf_ref[...]) * (1.0 + scf_ref[0]) + shf_ref[0]
    hfp_ref[...] = _pack_halves(hf)
    hh, hl = _split2(hf)
    rh, rl = _split2(rwt_ref[...])
    logit = _dot_nt(rh, hh) + _dot_nt(rl, hh) + _dot_nt(rh, hl)
    wd, msk = _route(logit, rb_ref[...])
    wd_ref[...] = wd
    msk_ref[...] = msk


def _wout(gla_o, proj, mla_o, gqa_o, xr, w, gout, gpm, gtm, gpf, shf, scf, rwt, rb, *, n_rows, tiles_per_batch, nb):
    tm = WOUT_TM
    mod_idx = lambda i: (jnp.minimum(i // tiles_per_batch, nb), 0, 0)
    row = lambda w_: pl.BlockSpec((tm, w_), lambda i: (i, 0))
    full = lambda a: pl.BlockSpec(a.shape, lambda i: (0,) * a.ndim)
    mspec = pl.BlockSpec((1, 1, D), mod_idx)
    return pl.pallas_call(
        _wout_kernel,
        grid=(n_rows // tm,),
        in_specs=[pl.BlockSpec((None, tm, 512), lambda i: (0, i, 0)),
                  pl.BlockSpec((None, tm, 512), lambda i: (1, i, 0)),
                  pl.BlockSpec((tm, 512), lambda i: (i, OFF_GR // 512)),
                  row(768), row(768), row(D), full(w), full(gout), full(gpm), mspec,
                  full(gpf), mspec, mspec, full(rwt), full(rb)],
        out_specs=[row(D), row(HALF), pl.BlockSpec((N_EXP, tm), lambda i: (0, i)),
                   pl.BlockSpec((N_EXP, tm), lambda i: (0, i))],
        out_shape=[jax.ShapeDtypeStruct((n_rows, D), F32), jax.ShapeDtypeStruct((n_rows, HALF), U32),
                   jax.ShapeDtypeStruct((N_EXP, n_rows), F32), jax.ShapeDtypeStruct((N_EXP, n_rows), I32)],
        compiler_params=_cp(("arbitrary",)),
        name="wout_route",
    )(gla_o, gla_o, proj, mla_o, gqa_o, xr, w, gout, gpm, gtm, gpf, shf, scf, rwt, rb)


MOE_TM = 256
DISP_TM = 256
COMB_TM = 128


def _dispatch_kernel(cnt_ref, pst_ref, ntl_ref, pos_ref, hfp_ref, xs_ref, zero_scr, sem, zsem):
    i = pl.program_id(0)

    def row_copy(t, k):
        p = pos_ref[0, 0, t * TOP_K + k]
        return pltpu.make_async_copy(hfp_ref.at[pl.ds(t, 1), :], xs_ref.at[pl.ds(p, 1), :], sem)

    def issue(t, carry):
        for k in range(TOP_K):
            row_copy(t, k).start()
        return carry

    lax.fori_loop(0, DISP_TM, issue, 0)

    @pl.when(i == 0)
    def _():
        zero_scr[...] = jnp.zeros_like(zero_scr)

        def per_expert(e, carry):
            cnt = cnt_ref[e]
            padlen = ntl_ref[e] * MOE_TM - cnt
            off = pst_ref[e] + cnt

            def zrow(r):
                return pltpu.make_async_copy(zero_scr.at[pl.ds(0, 1), :], xs_ref.at[pl.ds(off + r, 1), :], zsem)

            lax.fori_loop(0, padlen, lambda r, cr: (zrow(r).start(), cr)[1], 0)
            lax.fori_loop(0, padlen, lambda r, cr: (zrow(r).wait(), cr)[1], 0)
            return carry

        lax.fori_loop(0, N_EXP, per_expert, 0)

    def drain(t, carry):
        for k in range(TOP_K):
            row_copy(t, k).wait()
        return carry

    lax.fori_loop(0, DISP_TM, drain, 0)


def _dispatch(cnt, pst, ntl, pos_t, hfp, *, n_rows_sorted):
    n = hfp.shape[0]
    tm = DISP_TM
    grid_spec = pltpu.PrefetchScalarGridSpec(
        num_scalar_prefetch=3,
        grid=(n // tm,),
        in_specs=[pl.BlockSpec((1, 1, tm * TOP_K), lambda i, *_: (i, 0, 0), memory_space=pltpu.SMEM),
                  pl.BlockSpec((tm, HALF), lambda i, *_: (i, 0))],
        out_specs=pl.BlockSpec(memory_space=pl.ANY),
        scratch_shapes=[pltpu.VMEM((8, HALF), U32), pltpu.SemaphoreType.DMA, pltpu.SemaphoreType.DMA],
    )
    return pl.pallas_call(
        _dispatch_kernel,
        grid_spec=grid_spec,
        out_shape=jax.ShapeDtypeStruct((n_rows_sorted, HALF), U32),
        compiler_params=_cp(("arbitrary",)),
        name="moe_dispatch",
    )(cnt, pst, ntl, pos_t, hfp)


def _gmm_kernel(te_ref, nu_ref, x_ref, w1_ref, w3_ref, w2_ref, y_ref, w1b, w3b, w2b):
    i = pl.program_id(0)
    used = i < nu_ref[0]

    @pl.when(used)
    def _():
        changed = jnp.logical_or(i == 0, te_ref[jnp.maximum(i - 1, 0)] != te_ref[i])

        @pl.when(changed)
        def _():
            w1b[...] = w1_ref[0].astype(BF)
            w3b[...] = w3_ref[0].astype(BF)
            w2b[...] = w2_ref[0].astype(BF)

        lo, hi = _unpack_halves(x_ref[...])
        xa, xb = lo.astype(BF), hi.astype(BF)
        h1 = _dot(xa, w1b[0:HALF, :]) + _dot(xb, w1b[HALF:, :])
        h3 = _dot(xa, w3b[0:HALF, :]) + _dot(xb, w3b[HALF:, :])
        h = (h1 * _sigmoid(h1) * h3).astype(BF)
        y_ref[...] = _pack_halves(_dot(h, w2b[...]))

    @pl.when(jnp.logical_not(used))
    def _():
        y_ref[...] = jnp.zeros_like(y_ref)


def _gmm(tile_e, n_used, xs, w1, w3, w2):
    tm = MOE_TM
    n_tiles = xs.shape[0] // tm
    last = lambda i, nu: jnp.minimum(i, nu[0] - 1)
    grid_spec = pltpu.PrefetchScalarGridSpec(
        num_scalar_prefetch=2,
        grid=(n_tiles,),
        in_specs=[pl.BlockSpec((tm, HALF), lambda i, te, nu: (last(i, nu), 0)),
                  pl.BlockSpec((1, D, D_EXP), lambda i, te, nu: (te[last(i, nu)], 0, 0)),
                  pl.BlockSpec((1, D, D_EXP), lambda i, te, nu: (te[last(i, nu)], 0, 0)),
                  pl.BlockSpec((1, D_EXP, D), lambda i, te, nu: (te[last(i, nu)], 0, 0))],
        out_specs=pl.BlockSpec((tm, HALF), lambda i, te, nu: (i, 0)),
        scratch_shapes=[pltpu.VMEM((D, D_EXP), BF), pltpu.VMEM((D, D_EXP), BF), pltpu.VMEM((D_EXP, D), BF)],
    )
    return pl.pallas_call(
        _gmm_kernel,
        grid_spec=grid_spec,
        out_shape=jax.ShapeDtypeStruct(xs.shape, U32),
        compiler_params=_cp(("arbitrary",)),
        name="moe_experts",
    )(tile_e, n_used, xs, w1, w3, w2)


def _combine_kernel(pos_ref, w8_ref, hfp_ref, x_ref, s1_ref, s3_ref, s2_ref, gpost_ref, gtf_ref, ys_ref,
                    o_ref, ybuf, sem):
    tm = COMB_TM

    def row_copy(t, k):
        p = pos_ref[0, 0, t * TOP_K + k]
        return pltpu.make_async_copy(ys_ref.at[pl.ds(p, 1), :], ybuf.at[k, pl.ds(t, 1), :], sem)

    def issue(t, carry):
        for k in range(TOP_K):
            row_copy(t, k).start()
        return carry

    lax.fori_loop(0, tm, issue, 0)

    lo, hi = _unpack_halves(hfp_ref[...])
    xa, xb = lo.astype(BF), hi.astype(BF)
    h1 = _dot(xa, s1_ref[0:HALF, :]) + _dot(xb, s1_ref[HALF:, :])
    h3 = _dot(xa, s3_ref[0:HALF, :]) + _dot(xb, s3_ref[HALF:, :])
    sh = _dot((h1 * _sigmoid(h1) * h3).astype(BF), s2_ref[...])

    def drain(t, carry):
        for k in range(TOP_K):
            row_copy(t, k).wait()
        return carry

    lax.fori_loop(0, tm, drain, 0)

    acc_lo = sh[:, :HALF]
    acc_hi = sh[:, HALF:]
    w8 = w8_ref[...]
    for k in range(TOP_K):
        lo, hi = _unpack_halves(ybuf[k])
        wk = w8[:, k:k + 1]
        acc_lo = acc_lo + wk * lo
        acc_hi = acc_hi + wk * hi
    yf = jnp.concatenate([acc_lo, acc_hi], axis=1)
    o_ref[...] = x_ref[...] + gtf_ref[0] * _rms(yf, gpost_ref[...])


def _combine(pos_t, w8, hfp, xn, s1, s3, s2, gpost, gtf, ys, *, tiles_per_batch, nb):
    n = hfp.shape[0]
    tm = COMB_TM
    mod_idx = lambda i: (jnp.minimum(i // tiles_per_batch, nb), 0, 0)
    full = lambda a: pl.BlockSpec(a.shape, lambda i: (0,) * a.ndim)
    return pl.pallas_call(
        _combine_kernel,
        grid=(n // tm,),
        in_specs=[pl.BlockSpec((1, 1, tm * TOP_K), lambda i: (i, 0, 0), memory_space=pltpu.SMEM),
                  pl.BlockSpec((tm, TOP_K), lambda i: (i, 0)),
                  pl.BlockSpec((tm, HALF), lambda i: (i, 0)),
                  pl.BlockSpec((tm, D), lambda i: (i, 0)),
                  full(s1), full(s3), full(s2), full(gpost),
                  pl.BlockSpec((1, 1, D), mod_idx),
                  pl.BlockSpec(memory_space=pl.ANY)],
        out_specs=pl.BlockSpec((tm, D), lambda i: (i, 0)),
        out_shape=jax.ShapeDtypeStruct((n, D), F32),
        scratch_shapes=[pltpu.VMEM((TOP_K, tm, HALF), U32), pltpu.SemaphoreType.DMA],
        compiler_params=_cp(("arbitrary",)),
        name="moe_combine",
    )(pos_t, w8, hfp, xn, s1, s3, s2, gpost, gtf, ys)


def _dispatch_tables(msk, wd):
    n = msk.shape[1]
    counts = jnp.sum(msk, axis=1)
    rank = jnp.cumsum(msk, axis=1) - msk
    ntl = (counts + MOE_TM - 1) // MOE_TM
    tend = jnp.cumsum(ntl)
    pst = (tend - ntl) * MOE_TM
    dest = pst[:, None] + rank
    slot = jnp.cumsum(msk, axis=0) - msk
    onehot = (slot[None] == jnp.arange(TOP_K, dtype=I32)[:, None, None]) & (msk[None] > 0)
    pos = jnp.sum(jnp.where(onehot, dest[None], 0), axis=1)
    w8 = jnp.sum(jnp.where(onehot, wd[None], 0.0), axis=1)
    n_tiles = -(-(n * TOP_K) // MOE_TM) + N_EXP
    tile_e = jnp.minimum(jnp.searchsorted(tend, jnp.arange(n_tiles, dtype=I32), side='right'), N_EXP - 1)
    return (counts.astype(I32), pst.astype(I32), ntl.astype(I32), pos.T.astype(I32), w8.T,
            tile_e.astype(I32), tend[-1:].astype(I32), n_tiles)


def _inproj_weight(w_in):
    offs = np.cumsum((0, 256, 256, 512, 512, 16, 16, 384, 512, 64, 768, 256, 256))
    p = [w_in[:, offs[i]:offs[i + 1]] for i in range(12)]
    gq, gk, gv, gr, gzf, gzb, cq, ckv, kr, aq, ak, av = p
    krz = jnp.concatenate([kr, gzf, gzb, jnp.zeros((D, 32), w_in.dtype)], axis=1)
    return jnp.concatenate([gv, gr, ckv, aq, cq, krz, gq, gk, ak, av], axis=1).astype(BF)


def _mla_weights(w_uq, w_ukv):
    wq = w_uq.reshape(MLA_QR, MLA_H, MLA_NOPE + MLA_ROPE)
    wq = jnp.concatenate([wq, jnp.zeros((MLA_QR, MLA_H, 256 - MLA_NOPE - MLA_ROPE), wq.dtype)], axis=2)
    wkv = w_ukv.reshape(MLA_KVR, MLA_H, MLA_NOPE + MLA_DV)
    wk = wkv[:, :, :MLA_NOPE].reshape(MLA_KVR, MLA_H * MLA_NOPE)
    wv = wkv[:, :, MLA_NOPE:].reshape(MLA_KVR, MLA_H * MLA_DV)
    return wq.reshape(MLA_QR, MLA_H * 256).astype(BF), wk.astype(BF), wv.astype(BF)


def _gla_gate_weights(wf, bf, wb, bb):
    z = jnp.zeros((128, GLA_H * GLA_DK), F32)
    wg = jnp.stack([z.at[64:80].set(wf), z.at[80:96].set(wb)]).astype(BF)
    bg = jnp.stack([bf, bb]).reshape(2, 1, GLA_H * GLA_DK)
    return wg, bg


def kernel(x, c, ctx, c_ctx, w_mod, b_mod, g_pre_mix, g_post_mix, g_pre_ffn, g_post_ffn, w_in, gla_wg2_f, gla_bg_f, gla_wg2_b, gla_bg_b, gla_g_out, mla_g_q, mla_w_uq, mla_g_kv, mla_w_ukv, gqa_g_q, gqa_g_k, w_out, router_w, router_b, exp_w1, exp_w3, exp_w2, sh_w1, sh_w3, sh_w2):
    B, S, _ = x.shape
    depth = w_mod.shape[0]
    assert ctx.shape[1] == CT and S % 256 == 0 and B < 8
    RL, RC = B * S, B * CT
    R = RL + RC

    xr = jnp.concatenate([x.reshape(RL, D), ctx.reshape(RC, D)], axis=0)
    cvecs = jnp.concatenate([c, c_ctx[None], jnp.zeros((8 - B - 1, D), F32)], axis=0)
    mods = _modulation(cvecs, w_mod, b_mod)
    tabs = _rope_tables(S)
    ii = np.arange(GLA_CHUNK)
    tri = jnp.asarray(np.stack([ii[:, None] >= ii[None, :], ii[:, None] <= ii[None, :]]), BF)

    in_tm = 1024 if (S % 1024 == 0 and RC % 1024 == 0) else 256
    row2 = lambda v: v.reshape(1, -1)

    for l in range(depth):
        want_ctx = l < depth - 1
        m = mods[l].reshape(8, 6, 1, D)
        sh_m, sc_m, gt_m, sh_f, sc_f, gt_f = [m[:, i] for i in range(6)]

        proj = _inproj(xr, row2(g_pre_mix[l]), sh_m, sc_m, _inproj_weight(w_in[l]),
                       tm=in_tm, tiles_per_batch=S // in_tm, nb=B)
        wuq, wk, wv = _mla_weights(mla_w_uq[l], mla_w_ukv[l])
        qm, km, vm, qg, kg = _prep(proj, row2(mla_g_q[l]), row2(mla_g_kv[l]), wuq, wk, wv,
                                   row2(gqa_g_q[l]), row2(gqa_g_k[l]), tabs, S=S, RL=RL)
        wg, bg = _gla_gate_weights(gla_wg2_f[l], gla_bg_f[l], gla_wg2_b[l], gla_bg_b[l])
        gla_o = _gla(proj, wg, bg, tri, B=B, S=S, RL=RL)

        att = functools.partial(_attention, B=B, S=S, RL=RL)
        mla_o = att(qm, km, vm, hq=MLA_H, group=1, dk=256, dv=MLA_DV, vbase=0, ctx_queries=False, name="mla_lat")
        gqa_o = att(qg, kg, proj, hq=GQA_H, group=GQA_H // GQA_KV, dk=GQA_DH, dv=GQA_DH,
                    vbase=OFF_AV // GQA_DH, ctx_queries=False, name="gqa_lat")
        if want_ctx:
            mla_c = att(qm, km, vm, hq=MLA_H, group=1, dk=256, dv=MLA_DV, vbase=0, ctx_queries=True, name="mla_ctx")
            gqa_c = att(qg, kg, proj, hq=GQA_H, group=GQA_H // GQA_KV, dk=GQA_DH, dv=GQA_DH,
                        vbase=OFF_AV // GQA_DH, ctx_queries=True, name="gqa_ctx")
            mla_o = jnp.concatenate([mla_o, mla_c], axis=0)
            gqa_o = jnp.concatenate([gqa_o, gqa_c], axis=0)
        n_rows = R if want_ctx else RL

        xn, hfp, wd, msk = _wout(gla_o, proj, mla_o, gqa_o, xr, w_out[l].astype(BF), row2(gla_g_out[l]),
                                 row2(g_post_mix[l]), gt_m, row2(g_pre_ffn[l]), sh_f, sc_f,
                                 router_w[l].T, router_b[l].reshape(N_EXP, 1),
                                 n_rows=n_rows, tiles_per_batch=S // WOUT_TM, nb=B)

        cnt, pst, ntl, pos_t, w8, tile_e, n_used, n_tiles = _dispatch_tables(msk, wd)
        xs = _dispatch(cnt, pst, ntl, pos_t.reshape(n_rows // DISP_TM, 1, DISP_TM * TOP_K), hfp,
                       n_rows_sorted=n_tiles * MOE_TM)
        ys = _gmm(tile_e, n_used, xs, exp_w1[l], exp_w3[l], exp_w2[l])
        xo = _combine(pos_t.reshape(n_rows // COMB_TM, 1, COMB_TM * TOP_K), w8, hfp, xn,
                      sh_w1[l].astype(BF), sh_w3[l].astype(BF), sh_w2[l].astype(BF),
                      row2(g_post_ffn[l]), gt_f, ys, tiles_per_batch=S // COMB_TM, nb=B)
        xr = xo
    return xr[:RL].reshape(B, S, D)
```

```python
import functools
import math

import numpy as np
import jax
import jax.numpy as jnp
from jax import lax
from jax.experimental import pallas as pl
from jax.experimental.pallas import tpu as pltpu

F32 = jnp.float32
BF = jnp.bfloat16
PACKED = jnp.int32
I32 = jnp.int32

D = 2048
GRID_W = 64
CT = 256
ROPE_THETA = 10000.0
EPS = 1e-6
GLA_H, GLA_DK, GLA_DV, GLA_RANK, GLA_TAU, GLA_CHUNK = 4, 64, 128, 16, 16.0, 64
MLA_H, MLA_QR, MLA_KVR, MLA_NOPE, MLA_ROPE, MLA_DV = 6, 384, 512, 128, 64, 128
GQA_H, GQA_KV, GQA_DH = 6, 2, 128
N_EXP, TOP_K, N_GRP, TOPK_GRP, D_EXP, D_SH = 64, 8, 8, 4, 512, 512
ROUTED_SCALE = 2.5
LOG2E = math.log2(math.e)
HALF = D // 2

OFF_GV, OFF_GR, OFF_CKV, OFF_AQ, OFF_CQ, OFF_KRZ, OFF_GQ, OFF_GK, OFF_AK, OFF_AV = (
    0, 512, 1024, 1536, 2304, 2688, 2816, 3072, 3328, 3584)
PROJ_W = 3840
VMEM_LIMIT = 56 * 1024 * 1024


def _cp(sem, vmem=None):
    return pltpu.CompilerParams(dimension_semantics=sem, vmem_limit_bytes=vmem or VMEM_LIMIT)


def _dot(a, b):
    return jnp.dot(a, b, preferred_element_type=F32)


def _dot_nt(a, b):
    return lax.dot_general(a, b, (((1,), (1,)), ((), ())), preferred_element_type=F32)


def _dot_tn(a, b):
    return lax.dot_general(a, b, (((0,), (0,)), ((), ())), preferred_element_type=F32)


def _sigmoid(x):
    return 1.0 / (1.0 + jnp.exp(-x))


def _rms(x, g):
    return x * lax.rsqrt(jnp.mean(x * x, axis=-1, keepdims=True) + EPS) * g


def _split2(x):
    hi = x.astype(BF)
    return hi, (x - hi.astype(F32)).astype(BF)


def _pack_halves(y):
    return pltpu.pack_elementwise([y[:, :HALF], y[:, HALF:]], packed_dtype=BF)


def _unpack_halves(w):
    lo = pltpu.unpack_elementwise(w, index=0, packed_dtype=BF, unpacked_dtype=F32)
    hi = pltpu.unpack_elementwise(w, index=1, packed_dtype=BF, unpacked_dtype=F32)
    return lo, hi


def _mod_kernel(a_ref, w_ref, b_ref, o_ref):
    a = a_ref[...]
    a = a * _sigmoid(a)
    ah, al = _split2(a)
    wh, wl = _split2(w_ref[0])
    o_ref[0] = _dot(ah, wh) + _dot(al, wh) + _dot(ah, wl) + b_ref[0]


def _modulation(cvecs, w_mod, b_mod):
    L, _, n = w_mod.shape
    tn = 1024
    return pl.pallas_call(
        _mod_kernel,
        grid=(L, n // tn),
        in_specs=[pl.BlockSpec((8, D), lambda l, j: (0, 0)),
                  pl.BlockSpec((1, D, tn), lambda l, j: (l, 0, j)),
                  pl.BlockSpec((1, 1, tn), lambda l, j: (l, 0, j))],
        out_specs=pl.BlockSpec((1, 8, tn), lambda l, j: (l, 0, j)),
        out_shape=jax.ShapeDtypeStruct((L, 8, n), F32),
        compiler_params=_cp(("arbitrary", "arbitrary")),
        name="modulation",
    )(cvecs, w_mod, b_mod.reshape(L, 1, n))


def _inproj_kernel(x_ref, g_ref, sh_ref, sc_ref, w_ref, o_ref, a_scr):
    @pl.when(pl.program_id(1) == 0)
    def _():
        y = _rms(x_ref[...], g_ref[...])
        a_scr[...] = (y * (1.0 + sc_ref[0]) + sh_ref[0]).astype(BF)

    o_ref[...] = _dot(a_scr[...], w_ref[...]).astype(o_ref.dtype)


def _inproj(xr, g, sh, sc, w, *, tm, tiles_per_batch, nb):
    R = xr.shape[0]
    tn = 768
    mod_idx = lambda i, j: (jnp.minimum(i // tiles_per_batch, nb), 0, 0)
    return pl.pallas_call(
        _inproj_kernel,
        grid=(R // tm, PROJ_W // tn),
        in_specs=[pl.BlockSpec((tm, D), lambda i, j: (i, 0)),
                  pl.BlockSpec((1, D), lambda i, j: (0, 0)),
                  pl.BlockSpec((1, 1, D), mod_idx),
                  pl.BlockSpec((1, 1, D), mod_idx),
                  pl.BlockSpec((D, tn), lambda i, j: (0, j))],
        out_specs=pl.BlockSpec((tm, tn), lambda i, j: (i, j)),
        out_shape=jax.ShapeDtypeStruct((R, PROJ_W), BF),
        scratch_shapes=[pltpu.VMEM((tm, D), BF)],
        compiler_params=_cp(("arbitrary", "arbitrary")),
        name="inproj",
    )(xr, g, sh, sc, w)


PREP_TM = 256
QS_MLA = (MLA_NOPE + MLA_ROPE) ** -0.5 * LOG2E
QS_GQA = GQA_DH ** -0.5 * LOG2E


def _prep_kernel(cq_ref, ckv_ref, krz_ref, aq_ref, ak_ref, gq_ref, gkv_ref, wuq_ref, wk_ref, wv_ref,
                 gaq_ref, gak_ref, cg_ref, sg_ref, cm_ref, sm_ref,
                 qm_ref, km_ref, vm_ref, qg_ref, kg_ref):
    lane = lax.broadcasted_iota(I32, (PREP_TM, 128), 1)

    def rope(t, c, s, half):
        nxt = pltpu.roll(t, 128 - half, 1)
        prv = pltpu.roll(t, half, 1)
        return t * c + jnp.where((lane % (2 * half)) < half, nxt, prv) * s

    cm, sm = cm_ref[...], sm_ref[...]
    cg, sg = cg_ref[...], sg_ref[...]

    q = _dot(_rms(cq_ref[...].astype(F32), gq_ref[...]).astype(BF), wuq_ref[...])
    for h in range(MLA_H):
        a = 256 * h
        qm_ref[:, a:a + 128] = (q[:, a:a + 128] * QS_MLA).astype(BF)
        qm_ref[:, a + 128:a + 256] = (rope(q[:, a + 128:a + 256], cm, sm, MLA_ROPE // 4) * QS_MLA).astype(BF)

    n = _rms(ckv_ref[...].astype(F32), gkv_ref[...]).astype(BF)
    kn = _dot(n, wk_ref[...])
    vm_ref[...] = _dot(n, wv_ref[...]).astype(BF)
    kr = rope(krz_ref[...].astype(F32), cm, sm, MLA_ROPE // 4).astype(BF)
    for h in range(MLA_H):
        a = 256 * h
        km_ref[:, a:a + 128] = kn[:, 128 * h:128 * h + 128].astype(BF)
        km_ref[:, a + 128:a + 256] = kr

    for h in range(GQA_H):
        t = _rms(aq_ref[:, 128 * h:128 * h + 128].astype(F32), gaq_ref[...])
        qg_ref[:, 128 * h:128 * h + 128] = (rope(t, cg, sg, GQA_DH // 4) * QS_GQA).astype(BF)
    for h in range(GQA_KV):
        t = _rms(ak_ref[:, 128 * h:128 * h + 128].astype(F32), gak_ref[...])
        kg_ref[:, 128 * h:128 * h + 128] = rope(t, cg, sg, GQA_DH // 4).astype(BF)


def _rope_tables(S):
    pos = np.arange(S)
    rows, cols = pos // GRID_W, pos % GRID_W

    def tab(d, width):
        half = d // 2
        inv = ROPE_THETA ** (-np.arange(0, half, 2, dtype=np.float64) / half)
        ar, ac = rows[:, None] * inv[None], cols[:, None] * inv[None]
        c = np.concatenate([np.cos(ar), np.cos(ar), np.cos(ac), np.cos(ac)], 1)
        s = np.concatenate([-np.sin(ar), np.sin(ar), -np.sin(ac), np.sin(ac)], 1)
        ci = np.ones((CT, d)); si = np.zeros((CT, d))
        c = np.concatenate([c, ci], 0); s = np.concatenate([s, si], 0)
        pad = np.zeros((S + CT, width - d))
        return (jnp.asarray(np.concatenate([c, pad], 1), F32), jnp.asarray(np.concatenate([s, pad], 1), F32))

    return tab(GQA_DH, 128) + tab(MLA_ROPE, 128)


def _prep(proj, gq, gkv, wuq, wk, wv, gaq, gak, tabs, *, S, RL):
    R = proj.shape[0]
    tm = PREP_TM
    nlat = RL // tm
    spb = S // tm
    tab_idx = lambda i: (jnp.where(i < nlat, i % spb, spb), 0)
    col = lambda w, off: pl.BlockSpec((tm, w), lambda i: (i, off // w))
    full = lambda a: pl.BlockSpec(a.shape, lambda i: (0,) * a.ndim)
    tspec = pl.BlockSpec((tm, 128), tab_idx)
    widths = [MLA_H * 256, MLA_H * 256, MLA_H * MLA_DV, GQA_H * GQA_DH, GQA_KV * GQA_DH]
    return pl.pallas_call(
        _prep_kernel,
        grid=(R // tm,),
        in_specs=[col(MLA_QR, OFF_CQ), col(MLA_KVR, OFF_CKV), col(128, OFF_KRZ), col(768, OFF_AQ), col(256, OFF_AK),
                  full(gq), full(gkv), full(wuq), full(wk), full(wv), full(gaq), full(gak),
                  tspec, tspec, tspec, tspec],
        out_specs=[pl.BlockSpec((tm, w), lambda i: (i, 0)) for w in widths],
        out_shape=[jax.ShapeDtypeStruct((R, w), BF) for w in widths],
        compiler_params=_cp(("arbitrary",)),
        name="qk_prep",
    )(proj, proj, proj, proj, proj, gq, gkv, wuq, wk, wv, gaq, gak, *tabs)


GLA_TM = 256


def _gla_kernel(q_ref, k_ref, v_ref, z_ref, wg_ref, bg_ref, tri_ref, o_ref, st_scr, la_scr):
    d = pl.program_id(1)

    @pl.when(pl.program_id(2) == 0)
    def _():
        st_scr[...] = jnp.zeros_like(st_scr)

    z = _dot(z_ref[...], wg_ref[0]) + bg_ref[0]
    la_scr[...] = (jnp.minimum(z, 0.0) - jnp.log(1.0 + jnp.exp(-jnp.abs(z)))) * (1.0 / GLA_TAU)
    tri = tri_ref[0]
    keep = tri > 0
    C = GLA_CHUNK
    for cc in range(GLA_TM // C):
        c = cc + d * (GLA_TM // C - 1 - 2 * cc)
        r0 = pl.multiple_of(c * C, C)
        la = la_scr[pl.ds(r0, C), :]
        h1 = la.astype(BF)
        r1 = la - h1.astype(F32)
        h2 = r1.astype(BF)
        h3 = (r1 - h2.astype(F32)).astype(BF)
        b = _dot(tri, h1) + _dot(tri, h2) + _dot(tri, h3)
        tot = jnp.sum(la, axis=0, keepdims=True)
        q = q_ref[pl.ds(r0, C), :].astype(F32)
        k = k_ref[pl.ds(r0, C), :].astype(F32)
        qd = (q * (GLA_DK ** -0.5) * jnp.exp(b)).astype(BF)
        ki = (k * jnp.exp(-b)).astype(BF)
        ke = (k * jnp.exp(tot - b)).astype(BF)
        dec = jnp.exp(tot)
        v = v_ref[pl.ds(r0, C), :]
        for h in range(GLA_H):
            sl = slice(GLA_DK * h, GLA_DK * (h + 1))
            sv = slice(GLA_DV * h, GLA_DV * (h + 1))
            att = jnp.where(keep, _dot_nt(qd[:, sl], ki[:, sl]), 0.0).astype(BF)
            st = st_scr[h]
            o_ref[pl.ds(r0, C), sv] = _dot(att, v[:, sv]) + _dot_nt(qd[:, sl], st.astype(BF))
            st_scr[h] = st * dec[:, sl] + _dot_tn(v[:, sv], ke[:, sl])


def _gla(proj, wg, bg, tri, *, B, S, RL):
    R = proj.shape[0]
    tm = GLA_TM
    nj = S // tm + 1
    lat0 = lambda b: b * (S // tm)

    def rb(b, d, j):
        jl = jnp.where(d == 0, j - 1, S // tm - j)
        return jnp.where(j == 0, RL // tm + b, lat0(b) + jl)

    col = lambda w, off: pl.BlockSpec((tm, w), lambda b, d, j: (rb(b, d, j), off // w))
    sel = lambda a: pl.BlockSpec((1,) + a.shape[1:], lambda b, d, j: (d,) + (0,) * (a.ndim - 1))
    return pl.pallas_call(
        _gla_kernel,
        grid=(B, 2, nj),
        in_specs=[col(256, OFF_GQ), col(256, OFF_GK), col(512, OFF_GV), col(128, OFF_KRZ), sel(wg), sel(bg), sel(tri)],
        out_specs=pl.BlockSpec((None, tm, GLA_H * GLA_DV), lambda b, d, j: (d, rb(b, d, j), 0)),
        out_shape=jax.ShapeDtypeStruct((2, R, GLA_H * GLA_DV), F32),
        scratch_shapes=[pltpu.VMEM((GLA_H, GLA_DV, GLA_DK), F32), pltpu.VMEM((tm, GLA_H * GLA_DK), F32)],
        compiler_params=_cp(("arbitrary", "arbitrary", "arbitrary")),
        name="gla",
    )(proj, proj, proj, proj, wg, bg, tri)


def _flash_kernel(*refs, nseg, seg_lens, tq, dv, ck):
    q_ref = refs[0]
    k_refs = refs[1:1 + nseg]
    v_refs = refs[1 + nseg:1 + 2 * nseg]
    o_ref = refs[1 + 2 * nseg]
    s_scr = refs[2 + 2 * nseg]
    q = q_ref[...]
    m = jnp.full((tq, 128), -jnp.inf, F32)
    chunks = []
    off = 0
    for si in range(nseg):
        for c0 in range(0, seg_lens[si], ck):
            cl = min(ck, seg_lens[si] - c0)
            s = _dot_nt(q, k_refs[si][c0:c0 + cl, :])
            s_scr[:, off:off + cl] = s
            for l0 in range(0, cl, 128):
                m = jnp.maximum(m, s[:, l0:l0 + 128])
            chunks.append((si, c0, cl, off))
            off += cl
    mrow = jnp.max(m, axis=1, keepdims=True)
    l = jnp.zeros((tq, 128), F32)
    acc = jnp.zeros((tq, dv), F32)
    for si, c0, cl, off in chunks:
        p = jnp.exp2(s_scr[:, off:off + cl] - mrow)
        for l0 in range(0, cl, 128):
            l = l + p[:, l0:l0 + 128]
        acc = acc + _dot(p.astype(BF), v_refs[si][c0:c0 + cl, :])
    o_ref[...] = (acc / jnp.sum(l, axis=1, keepdims=True)).astype(o_ref.dtype)


def _attention(qa, ka, va, *, B, S, RL, hq, group, dk, dv, vbase, ctx_queries, name):
    cb = RL // CT
    kh = lambda h: h // group
    if ctx_queries:
        tq, nq, n_out = CT, 1, B * CT
        q_spec = pl.BlockSpec((tq, dk), lambda b, h, i: (cb + b, h))
        o_spec = pl.BlockSpec((tq, dv), lambda b, h, i: (b, h))
        k_specs = [pl.BlockSpec((CT, dk), lambda b, h, i: (cb + b, kh(h)))]
        v_specs = [pl.BlockSpec((CT, dv), lambda b, h, i: (cb + b, vbase + kh(h)))]
        seg_lens = (CT,)
    else:
        tq, nq, n_out = 256, S // 256, RL
        q_spec = pl.BlockSpec((tq, dk), lambda b, h, i: (b * nq + i, h))
        o_spec = pl.BlockSpec((tq, dv), lambda b, h, i: (b * nq + i, h))
        k_specs = [pl.BlockSpec((CT, dk), lambda b, h, i: (cb + b, kh(h))),
                   pl.BlockSpec((S, dk), lambda b, h, i: (b, kh(h)))]
        v_specs = [pl.BlockSpec((CT, dv), lambda b, h, i: (cb + b, vbase + kh(h))),
                   pl.BlockSpec((S, dv), lambda b, h, i: (b, vbase + kh(h)))]
        seg_lens = (CT, S)
    nseg = len(seg_lens)
    kern = functools.partial(_flash_kernel, nseg=nseg, seg_lens=seg_lens, tq=tq, dv=dv, ck=512)
    return pl.pallas_call(
        kern,
        grid=(B, hq, nq),
        in_specs=[q_spec] + k_specs + v_specs,
        out_specs=o_spec,
        out_shape=jax.ShapeDtypeStruct((n_out, hq * dv), BF),
        scratch_shapes=[pltpu.VMEM((tq, sum(seg_lens)), F32)],
        compiler_params=_cp(("arbitrary", "arbitrary", "arbitrary")),
        name=name,
    )(qa, *([ka] * nseg), *([va] * nseg))


WOUT_TM = 256


def _route(logit, rb):
    T = logit.shape[1]
    gsz = N_EXP // N_GRP
    scores = _sigmoid(logit)
    sel = scores + rb
    ninf = jnp.float32(-jnp.inf)
    sel3 = sel.reshape(N_GRP, gsz, T)
    idx3 = lax.broadcasted_iota(I32, (N_GRP, gsz, T), 1)
    m1 = jnp.max(sel3, axis=1, keepdims=True)
    first = jnp.min(jnp.where(sel3 == m1, idx3, gsz), axis=1, keepdims=True)
    m2 = jnp.max(jnp.where(idx3 == first, ninf, sel3), axis=1, keepdims=True)
    gs = (m1 + m2).reshape(N_GRP, T)
    gi = lax.broadcasted_iota(I32, (N_GRP, T), 0)
    grank = jnp.zeros((N_GRP, T), I32)
    for j in range(N_GRP):
        row = gs[j:j + 1, :]
        grank = grank + ((row > gs) | ((row == gs) & (j < gi))).astype(I32)
    gkeep = (grank < TOPK_GRP).reshape(N_GRP, 1, T)
    selm = jnp.where(jnp.broadcast_to(gkeep, (N_GRP, gsz, T)), sel3, ninf).reshape(N_EXP, T)
    ei = lax.broadcasted_iota(I32, (N_EXP, T), 0)
    erank = jnp.zeros((N_EXP, T), I32)
    for j in range(N_EXP):
        row = selm[j:j + 1, :]
        erank = erank + ((row > selm) | ((row == selm) & (j < ei))).astype(I32)
    chosen = erank < TOP_K
    w = jnp.where(chosen, scores, 0.0)
    wd = w / jnp.sum(w, axis=0, keepdims=True) * ROUTED_SCALE
    return wd, chosen


def _store_tile_rows(ref, packed, tm):
    for cblk in range(HALF // 128):
        ref[pl.ds(cblk, tm, stride=8), :] = packed[:, 128 * cblk:128 * (cblk + 1)]


def _load_tile_rows(ref, tm):
    los, his = [], []
    for cblk in range(HALF // 128):
        lo, hi = _unpack_halves(ref[pl.ds(cblk, tm, stride=8), :])
        los.append(lo.astype(BF))
        his.append(hi.astype(BF))
    return jnp.concatenate(los, axis=1), jnp.concatenate(his, axis=1)


def _wout_kernel(of_ref, ob_ref, gr_ref, mla_ref, gqa_ref, x_ref, w_ref, gout_ref, gpm_ref, gtm_ref,
                 gpf_ref, shf_ref, scf_ref, rwt_ref, rb_ref, ust_ref, lst_ref,
                 xn_ref, hfp_ref, r8_ref, e8_ref, w8_ref, cnt_ref, cnt_scr):
    @pl.when(pl.program_id(0) == 0)
    def _():
        cnt_scr[...] = jnp.zeros_like(cnt_scr)

    og = of_ref[...] + ob_ref[...]
    gr = gr_ref[...].astype(F32)
    parts = []
    for h in range(GLA_H):
        sv = slice(GLA_DV * h, GLA_DV * (h + 1))
        g = gr[:, sv]
        parts.append((_rms(og[:, sv], gout_ref[...]) * (g * _sigmoid(g))).astype(BF))
    gla = jnp.concatenate(parts, axis=1)
    n_g, n_m = GLA_H * GLA_DV, MLA_H * MLA_DV
    y = (_dot(gla, w_ref[0:n_g, :]) + _dot(mla_ref[...], w_ref[n_g:n_g + n_m, :])
         + _dot(gqa_ref[...], w_ref[n_g + n_m:, :]))
    xn = x_ref[...] + gtm_ref[0] * _rms(y, gpm_ref[...])
    xn_ref[...] = xn
    hf = _rms(xn, gpf_ref[...]) * (1.0 + scf_ref[0]) + shf_ref[0]
    _store_tile_rows(hfp_ref, _pack_halves(hf), WOUT_TM)
    hh, hl = _split2(hf)
    rh, rl = _split2(rwt_ref[...])
    logit = _dot_nt(rh, hh) + _dot_nt(rl, hh) + _dot_nt(rh, hl)
    wd, chosen = _route(logit, rb_ref[...])

    cb = jnp.where(chosen, 1.0, 0.0)
    before = _dot(cb.astype(BF), ust_ref[...])
    rank = cnt_scr[:, 0:1] + before
    slot = _dot(lst_ref[...], cb.astype(BF))
    eid = lax.broadcasted_iota(I32, chosen.shape, 0).astype(F32)
    r8, e8, w8 = [], [], []
    for k in range(TOP_K):
        pick = chosen & (slot == float(k))
        r8.append(jnp.sum(jnp.where(pick, rank, 0.0), axis=0, keepdims=True))
        e8.append(jnp.sum(jnp.where(pick, eid, 0.0), axis=0, keepdims=True))
        w8.append(jnp.sum(jnp.where(pick, wd, 0.0), axis=0, keepdims=True))
    r8_ref[...] = jnp.concatenate(r8, axis=0).astype(I32)
    e8_ref[...] = jnp.concatenate(e8, axis=0).astype(I32)
    w8_ref[...] = jnp.concatenate(w8, axis=0)
    cnt_scr[...] = cnt_scr[...] + jnp.sum(cb, axis=1, keepdims=True)
    cnt_ref[...] = cnt_scr[...].astype(I32)


def _wout(gla_o, proj, mla_o, gqa_o, xr, w, gout, gpm, gtm, gpf, shf, scf, rwt, rb, *, n_rows, tiles_per_batch, nb):
    tm = WOUT_TM
    mod_idx = lambda i: (jnp.minimum(i // tiles_per_batch, nb), 0, 0)
    row = lambda w_: pl.BlockSpec((tm, w_), lambda i: (i, 0))
    full = lambda a: pl.BlockSpec(a.shape, lambda i: (0,) * a.ndim)
    mspec = pl.BlockSpec((1, 1, D), mod_idx)
    it = np.arange(tm)
    ust = jnp.asarray(it[:, None] < it[None, :], BF)
    ie = np.arange(N_EXP)
    lst = jnp.asarray(ie[None, :] < ie[:, None], BF)
    k8 = pl.BlockSpec((TOP_K, tm), lambda i: (0, i))
    return pl.pallas_call(
        _wout_kernel,
        grid=(n_rows // tm,),
        in_specs=[pl.BlockSpec((None, tm, 512), lambda i: (0, i, 0)),
                  pl.BlockSpec((None, tm, 512), lambda i: (1, i, 0)),
                  pl.BlockSpec((tm, 512), lambda i: (i, OFF_GR // 512)),
                  row(768), row(768), row(D), full(w), full(gout), full(gpm), mspec,
                  full(gpf), mspec, mspec, full(rwt), full(rb), full(ust), full(lst)],
        out_specs=[row(D), pl.BlockSpec((tm * 8, 128), lambda i: (i, 0)), k8, k8, k8,
                   pl.BlockSpec((N_EXP, 128), lambda i: (0, 0))],
        out_shape=[jax.ShapeDtypeStruct((n_rows, D), F32), jax.ShapeDtypeStruct((n_rows * 8, 128), PACKED),
                   jax.ShapeDtypeStruct((TOP_K, n_rows), I32), jax.ShapeDtypeStruct((TOP_K, n_rows), I32),
                   jax.ShapeDtypeStruct((TOP_K, n_rows), F32), jax.ShapeDtypeStruct((N_EXP, 128), I32)],
        scratch_shapes=[pltpu.VMEM((N_EXP, 128), F32)],
        compiler_params=_cp(("arbitrary",)),
        name="wout_route",
    )(gla_o, gla_o, proj, mla_o, gqa_o, xr, w, gout, gpm, gtm, gpf, shf, scf, rwt, rb, ust, lst)


MOE_TM = 256
DISP_TM = 256
COMB_TM = 128


def _tile_row(ref, r):
    return ref.at[pl.ds(pl.multiple_of(r * 8, 8), 8), :]


def _dispatch_kernel(cnt_ref, pst_ref, ntl_ref, nu_ref, pos_ref, hfp_ref, xs_ref, zero_scr, sem, zsem):
    i = pl.program_id(0)
    tm = DISP_TM
    tile_rows = MOE_TM * 8
    n_tiles = xs_ref.shape[0] // tile_rows

    def issue(t, carry):
        for k in range(TOP_K):
            p = pos_ref[0, 0, k * tm + t]
            pltpu.make_async_copy(_tile_row(hfp_ref, t), _tile_row(xs_ref, p), sem).start()
        return carry

    lax.fori_loop(0, tm, issue, 0, unroll=2)

    @pl.when(i == 0)
    def _():
        zero_scr[...] = jnp.zeros_like(zero_scr)

        def per_expert(e, carry):
            cnt = cnt_ref[e]
            padlen = ntl_ref[e] * MOE_TM - cnt
            off = pst_ref[e] + cnt

            def zrow(r):
                return pltpu.make_async_copy(zero_scr.at[pl.ds(0, 8), :], _tile_row(xs_ref, off + r), zsem)

            lax.fori_loop(0, padlen, lambda r, cr: (zrow(r).start(), cr)[1], 0)
            lax.fori_loop(0, padlen, lambda r, cr: (zrow(r).wait(), cr)[1], 0)
            return carry

        lax.fori_loop(0, N_EXP, per_expert, 0)

        def ztile(t):
            dst = xs_ref.at[pl.ds(pl.multiple_of(t * tile_rows, tile_rows), tile_rows), :]
            return pltpu.make_async_copy(zero_scr, dst, zsem)

        lax.fori_loop(nu_ref[0], n_tiles, lambda t, cr: (ztile(t).start(), cr)[1], 0)
        lax.fori_loop(nu_ref[0], n_tiles, lambda t, cr: (ztile(t).wait(), cr)[1], 0)

    for k in range(TOP_K):
        pltpu.make_async_copy(hfp_ref, xs_ref.at[pl.ds(0, tm * 8), :], sem).wait()


def _dispatch(cnt, pst, ntl, n_used, pos_t, hfp, *, n_rows_sorted):
    n = hfp.shape[0] // 8
    tm = DISP_TM
    grid_spec = pltpu.PrefetchScalarGridSpec(
        num_scalar_prefetch=4,
        grid=(n // tm,),
        in_specs=[pl.BlockSpec((1, 1, tm * TOP_K), lambda i, *_: (i, 0, 0), memory_space=pltpu.SMEM),
                  pl.BlockSpec((tm * 8, 128), lambda i, *_: (i, 0))],
        out_specs=pl.BlockSpec(memory_space=pl.ANY),
        scratch_shapes=[pltpu.VMEM((MOE_TM * 8, 128), PACKED), pltpu.SemaphoreType.DMA, pltpu.SemaphoreType.DMA],
    )
    return pl.pallas_call(
        _dispatch_kernel,
        grid_spec=grid_spec,
        out_shape=jax.ShapeDtypeStruct((n_rows_sorted * 8, 128), PACKED),
        compiler_params=_cp(("arbitrary",)),
        name="moe_dispatch",
    )(cnt, pst, ntl, n_used, pos_t, hfp)


def _gmm_kernel(te_ref, nu_ref, x_ref, w1_ref, w3_ref, w2_ref, y_ref, w1b, w3b, w2b):
    i = pl.program_id(0)
    used = i < nu_ref[0]

    @pl.when(used)
    def _():
        changed = jnp.logical_or(i == 0, te_ref[jnp.maximum(i - 1, 0)] != te_ref[i])

        @pl.when(changed)
        def _():
            w1b[...] = w1_ref[0].astype(BF)
            w3b[...] = w3_ref[0].astype(BF)
            w2b[...] = w2_ref[0].astype(BF)

        xa, xb = _load_tile_rows(x_ref, MOE_TM)
        h1 = _dot(xa, w1b[0:HALF, :]) + _dot(xb, w1b[HALF:, :])
        h3 = _dot(xa, w3b[0:HALF, :]) + _dot(xb, w3b[HALF:, :])
        h = (h1 * _sigmoid(h1) * h3).astype(BF)
        _store_tile_rows(y_ref, _pack_halves(_dot(h, w2b[...])), MOE_TM)

    @pl.when(jnp.logical_not(used))
    def _():
        y_ref[...] = jnp.zeros_like(y_ref)


def _gmm(tile_e, n_used, xs, w1, w3, w2, layer):
    tm = MOE_TM
    n_tiles = xs.shape[0] // (tm * 8)
    last = lambda i, nu: jnp.minimum(i, nu[0] - 1)
    wsel = lambda i, te, nu: (layer * N_EXP + te[last(i, nu)], 0, 0)
    grid_spec = pltpu.PrefetchScalarGridSpec(
        num_scalar_prefetch=2,
        grid=(n_tiles,),
        in_specs=[pl.BlockSpec((tm * 8, 128), lambda i, te, nu: (last(i, nu), 0)),
                  pl.BlockSpec((1, D, D_EXP), wsel),
                  pl.BlockSpec((1, D, D_EXP), wsel),
                  pl.BlockSpec((1, D_EXP, D), wsel)],
        out_specs=pl.BlockSpec((tm * 8, 128), lambda i, te, nu: (i, 0)),
        scratch_shapes=[pltpu.VMEM((D, D_EXP), BF), pltpu.VMEM((D, D_EXP), BF), pltpu.VMEM((D_EXP, D), BF)],
    )
    return pl.pallas_call(
        _gmm_kernel,
        grid_spec=grid_spec,
        out_shape=jax.ShapeDtypeStruct(xs.shape, PACKED),
        compiler_params=_cp(("arbitrary",)),
        name="moe_experts",
    )(tile_e, n_used, xs, w1, w3, w2)


def _combine_kernel(pos_ref, w8_ref, hfp_ref, x_ref, s1_ref, s3_ref, s2_ref, gpost_ref, gtf_ref, ys_ref,
                    o_ref, ybuf, sem):
    tm = COMB_TM

    def issue(t, carry):
        for k in range(TOP_K):
            p = pos_ref[0, 0, k * tm + t]
            pltpu.make_async_copy(_tile_row(ys_ref, p), _tile_row(ybuf.at[k], t), sem).start()
        return carry

    lax.fori_loop(0, tm, issue, 0, unroll=2)

    xa, xb = _load_tile_rows(hfp_ref, tm)
    h1 = _dot(xa, s1_ref[0:HALF, :]) + _dot(xb, s1_ref[HALF:, :])
    h3 = _dot(xa, s3_ref[0:HALF, :]) + _dot(xb, s3_ref[HALF:, :])
    sh = _dot((h1 * _sigmoid(h1) * h3).astype(BF), s2_ref[...])

    for k in range(TOP_K):
        pltpu.make_async_copy(ys_ref.at[pl.ds(0, tm * 8), :], ybuf.at[k], sem).wait()

    w8 = w8_ref[...]
    wk = [w8[:, k:k + 1] for k in range(TOP_K)]
    los, his = [], []
    for cblk in range(HALF // 128):
        a_lo = sh[:, 128 * cblk:128 * (cblk + 1)]
        a_hi = sh[:, HALF + 128 * cblk:HALF + 128 * (cblk + 1)]
        for k in range(TOP_K):
            lo, hi = _unpack_halves(ybuf[k, pl.ds(cblk, tm, stride=8), :])
            a_lo = a_lo + wk[k] * lo
            a_hi = a_hi + wk[k] * hi
        los.append(a_lo)
        his.append(a_hi)
    yf = jnp.concatenate(los + his, axis=1)
    o_ref[...] = x_ref[...] + gtf_ref[0] * _rms(yf, gpost_ref[...])


def _combine(pos_t, w8, hfp, xn, s1, s3, s2, gpost, gtf, ys, *, tiles_per_batch, nb):
    n = hfp.shape[0] // 8
    tm = COMB_TM
    mod_idx = lambda i: (jnp.minimum(i // tiles_per_batch, nb), 0, 0)
    full = lambda a: pl.BlockSpec(a.shape, lambda i: (0,) * a.ndim)
    return pl.pallas_call(
        _combine_kernel,
        grid=(n // tm,),
        in_specs=[pl.BlockSpec((1, 1, tm * TOP_K), lambda i: (i, 0, 0), memory_space=pltpu.SMEM),
                  pl.BlockSpec((tm, TOP_K), lambda i: (i, 0)),
                  pl.BlockSpec((tm * 8, 128), lambda i: (i, 0)),
                  pl.BlockSpec((tm, D), lambda i: (i, 0)),
                  full(s1), full(s3), full(s2), full(gpost),
                  pl.BlockSpec((1, 1, D), mod_idx),
                  pl.BlockSpec(memory_space=pl.ANY)],
        out_specs=pl.BlockSpec((tm, D), lambda i: (i, 0)),
        out_shape=jax.ShapeDtypeStruct((n, D), F32),
        scratch_shapes=[pltpu.VMEM((TOP_K, tm * 8, 128), PACKED), pltpu.SemaphoreType.DMA],
        compiler_params=_cp(("arbitrary",)),
        name="moe_combine",
    )(pos_t, w8, hfp, xn, s1, s3, s2, gpost, gtf, ys)


def _dispatch_tables(counts, r8, e8):
    n = r8.shape[1]
    ntl = (counts + MOE_TM - 1) // MOE_TM
    tend = jnp.cumsum(ntl)
    pst = (tend - ntl) * MOE_TM
    onehot = e8[None] == jnp.arange(N_EXP, dtype=I32)[:, None, None]
    pos = r8 + jnp.sum(jnp.where(onehot, pst[:, None, None], 0), axis=0)
    n_tiles = -(-(n * TOP_K) // MOE_TM) + N_EXP
    tile_e = jnp.sum((jnp.arange(n_tiles, dtype=I32)[:, None] >= tend[None, :]).astype(I32), axis=1)
    tile_e = jnp.minimum(tile_e, N_EXP - 1)
    return pst.astype(I32), ntl.astype(I32), pos.astype(I32), tile_e.astype(I32), tend[-1:].astype(I32), n_tiles


def _pos_blocks(pos, tm):
    k, n = pos.shape
    return pos.reshape(k, n // tm, tm).transpose(1, 0, 2).reshape(n // tm, 1, k * tm)


def _inproj_weight(w_in):
    offs = np.cumsum((0, 256, 256, 512, 512, 16, 16, 384, 512, 64, 768, 256, 256))
    p = [w_in[:, offs[i]:offs[i + 1]] for i in range(12)]
    gq, gk, gv, gr, gzf, gzb, cq, ckv, kr, aq, ak, av = p
    krz = jnp.concatenate([kr, gzf, gzb, jnp.zeros((D, 32), w_in.dtype)], axis=1)
    return jnp.concatenate([gv, gr, ckv, aq, cq, krz, gq, gk, ak, av], axis=1).astype(BF)


def _mla_weights(w_uq, w_ukv):
    wq = w_uq.reshape(MLA_QR, MLA_H, MLA_NOPE + MLA_ROPE)
    wq = jnp.concatenate([wq, jnp.zeros((MLA_QR, MLA_H, 256 - MLA_NOPE - MLA_ROPE), wq.dtype)], axis=2)
    wkv = w_ukv.reshape(MLA_KVR, MLA_H, MLA_NOPE + MLA_DV)
    wk = wkv[:, :, :MLA_NOPE].reshape(MLA_KVR, MLA_H * MLA_NOPE)
    wv = wkv[:, :, MLA_NOPE:].reshape(MLA_KVR, MLA_H * MLA_DV)
    return wq.reshape(MLA_QR, MLA_H * 256).astype(BF), wk.astype(BF), wv.astype(BF)


def _gla_gate_weights(wf, bf, wb, bb):
    z = jnp.zeros((128, GLA_H * GLA_DK), F32)
    wg = jnp.stack([z.at[64:80].set(wf), z.at[80:96].set(wb)]).astype(BF)
    bg = jnp.stack([bf, bb]).reshape(2, 1, GLA_H * GLA_DK)
    return wg, bg


def kernel(x, c, ctx, c_ctx, w_mod, b_mod, g_pre_mix, g_post_mix, g_pre_ffn, g_post_ffn, w_in, gla_wg2_f, gla_bg_f, gla_wg2_b, gla_bg_b, gla_g_out, mla_g_q, mla_w_uq, mla_g_kv, mla_w_ukv, gqa_g_q, gqa_g_k, w_out, router_w, router_b, exp_w1, exp_w3, exp_w2, sh_w1, sh_w3, sh_w2):
    B, S, _ = x.shape
    depth = w_mod.shape[0]
    assert ctx.shape[1] == CT and S % 256 == 0 and B < 8
    RL, RC = B * S, B * CT
    R = RL + RC

    xr = jnp.concatenate([x.reshape(RL, D), ctx.reshape(RC, D)], axis=0)
    cvecs = jnp.concatenate([c, c_ctx[None], jnp.zeros((8 - B - 1, D), F32)], axis=0)
    mods = _modulation(cvecs, w_mod, b_mod)
    tabs = _rope_tables(S)
    ii = np.arange(GLA_CHUNK)
    tri = jnp.asarray(np.stack([ii[:, None] >= ii[None, :], ii[:, None] <= ii[None, :]]), BF)

    in_tm = 1024 if (S % 1024 == 0 and RC % 1024 == 0) else 256
    row2 = lambda v: v.reshape(1, -1)
    ew1 = exp_w1.reshape(depth * N_EXP, D, D_EXP)
    ew3 = exp_w3.reshape(depth * N_EXP, D, D_EXP)
    ew2 = exp_w2.reshape(depth * N_EXP, D_EXP, D)

    for l in range(depth):
        want_ctx = l < depth - 1
        m = mods[l].reshape(8, 6, 1, D)
        sh_m, sc_m, gt_m, sh_f, sc_f, gt_f = [m[:, i] for i in range(6)]

        proj = _inproj(xr, row2(g_pre_mix[l]), sh_m, sc_m, _inproj_weight(w_in[l]),
                       tm=in_tm, tiles_per_batch=S // in_tm, nb=B)
        wuq, wk, wv = _mla_weights(mla_w_uq[l], mla_w_ukv[l])
        qm, km, vm, qg, kg = _prep(proj, row2(mla_g_q[l]), row2(mla_g_kv[l]), wuq, wk, wv,
                                   row2(gqa_g_q[l]), row2(gqa_g_k[l]), tabs, S=S, RL=RL)
        wg, bg = _gla_gate_weights(gla_wg2_f[l], gla_bg_f[l], gla_wg2_b[l], gla_bg_b[l])
        gla_o = _gla(proj, wg, bg, tri, B=B, S=S, RL=RL)

        att = functools.partial(_attention, B=B, S=S, RL=RL)
        mla = functools.partial(att, qm, km, vm, hq=MLA_H, group=1, dk=256, dv=MLA_DV, vbase=0)
        gqa = functools.partial(att, qg, kg, proj, hq=GQA_H, group=GQA_H // GQA_KV, dk=GQA_DH, dv=GQA_DH,
                                vbase=OFF_AV // GQA_DH)
        mla_o = mla(ctx_queries=False, name="mla_lat")
        gqa_o = gqa(ctx_queries=False, name="gqa_lat")
        if want_ctx:
            mla_o = jnp.concatenate([mla_o, mla(ctx_queries=True, name="mla_ctx")], axis=0)
            gqa_o = jnp.concatenate([gqa_o, gqa(ctx_queries=True, name="gqa_ctx")], axis=0)
        n_rows = R if want_ctx else RL

        xn, hfp, r8, e8, w8, cnt = _wout(gla_o, proj, mla_o, gqa_o, xr, w_out[l].astype(BF), row2(gla_g_out[l]),
                                         row2(g_post_mix[l]), gt_m, row2(g_pre_ffn[l]), sh_f, sc_f,
                                         router_w[l].T, router_b[l].reshape(N_EXP, 1),
                                         n_rows=n_rows, tiles_per_batch=S // WOUT_TM, nb=B)

        counts = cnt[:, 0]
        pst, ntl, pos, tile_e, n_used, n_tiles = _dispatch_tables(counts, r8, e8)
        xs = _dispatch(counts, pst, ntl, n_used, _pos_blocks(pos, DISP_TM), hfp, n_rows_sorted=n_tiles * MOE_TM)
        ys = _gmm(tile_e, n_used, xs, ew1, ew3, ew2, l)
        xo = _combine(_pos_blocks(pos, COMB_TM), w8.T, hfp, xn,
                      sh_w1[l].astype(BF), sh_w3[l].astype(BF), sh_w2[l].astype(BF),
                      row2(g_post_ffn[l]), gt_f, ys, tiles_per_batch=S // COMB_TM, nb=B)
        xr = xo
    return xr[:RL].reshape(B, S, D)
```

```python
import functools
import math

import numpy as np
import jax
import jax.numpy as jnp
from jax import lax
from jax.experimental import pallas as pl
from jax.experimental.pallas import tpu as pltpu

F32 = jnp.float32
BF = jnp.bfloat16
PACKED = jnp.uint32
I32 = jnp.int32

D = 2048
GRID_W = 64
CT = 256
ROPE_THETA = 10000.0
EPS = 1e-6
GLA_H, GLA_DK, GLA_DV, GLA_RANK, GLA_TAU, GLA_CHUNK = 4, 64, 128, 16, 16.0, 64
MLA_H, MLA_QR, MLA_KVR, MLA_NOPE, MLA_ROPE, MLA_DV = 6, 384, 512, 128, 64, 128
GQA_H, GQA_KV, GQA_DH = 6, 2, 128
N_EXP, TOP_K, N_GRP, TOPK_GRP, D_EXP, D_SH = 64, 8, 8, 4, 512, 512
ROUTED_SCALE = 2.5
LOG2E = math.log2(math.e)
HALF = D // 2

OFF_GV, OFF_GR, OFF_CKV, OFF_AQ, OFF_CQ, OFF_KRZ, OFF_GQ, OFF_GK, OFF_AK, OFF_AV = (
    0, 512, 1024, 1536, 2304, 2688, 2816, 3072, 3328, 3584)
PROJ_W = 3840
VMEM_LIMIT = 56 * 1024 * 1024


def _cp(sem, vmem=None):
    return pltpu.CompilerParams(dimension_semantics=sem, vmem_limit_bytes=vmem or VMEM_LIMIT)


def _dot(a, b):
    return jnp.dot(a, b, preferred_element_type=F32)


def _dot_nt(a, b):
    return lax.dot_general(a, b, (((1,), (1,)), ((), ())), preferred_element_type=F32)


def _dot_tn(a, b):
    return lax.dot_general(a, b, (((0,), (0,)), ((), ())), preferred_element_type=F32)


def _sigmoid(x):
    return 1.0 / (1.0 + jnp.exp(-x))


def _rms(x, g):
    return x * lax.rsqrt(jnp.mean(x * x, axis=-1, keepdims=True) + EPS) * g


def _split2(x):
    hi = x.astype(BF)
    return hi, (x - hi.astype(F32)).astype(BF)


def _pack_halves(y):
    return pltpu.pack_elementwise([y[:, :HALF], y[:, HALF:]], packed_dtype=BF)


def _unpack_halves(w):
    lo = pltpu.unpack_elementwise(w, index=0, packed_dtype=BF, unpacked_dtype=F32)
    hi = pltpu.unpack_elementwise(w, index=1, packed_dtype=BF, unpacked_dtype=F32)
    return lo, hi


def _mod_kernel(a_ref, w_ref, b_ref, o_ref):
    a = a_ref[...]
    a = a * _sigmoid(a)
    ah, al = _split2(a)
    wh, wl = _split2(w_ref[0])
    o_ref[0] = _dot(ah, wh) + _dot(al, wh) + _dot(ah, wl) + b_ref[0]


def _modulation(cvecs, w_mod, b_mod):
    L, _, n = w_mod.shape
    tn = 1024
    return pl.pallas_call(
        _mod_kernel,
        grid=(L, n // tn),
        in_specs=[pl.BlockSpec((8, D), lambda l, j: (0, 0)),
                  pl.BlockSpec((1, D, tn), lambda l, j: (l, 0, j)),
                  pl.BlockSpec((1, 1, tn), lambda l, j: (l, 0, j))],
        out_specs=pl.BlockSpec((1, 8, tn), lambda l, j: (l, 0, j)),
        out_shape=jax.ShapeDtypeStruct((L, 8, n), F32),
        compiler_params=_cp(("arbitrary", "arbitrary")),
        name="modulation",
    )(cvecs, w_mod, b_mod.reshape(L, 1, n))


def _inproj_kernel(x_ref, g_ref, sh_ref, sc_ref, w_ref, o_ref, a_scr):
    @pl.when(pl.program_id(1) == 0)
    def _():
        y = _rms(x_ref[...], g_ref[...])
        a_scr[...] = (y * (1.0 + sc_ref[0]) + sh_ref[0]).astype(BF)

    o_ref[...] = _dot(a_scr[...], w_ref[...]).astype(o_ref.dtype)


def _inproj(xr, g, sh, sc, w, *, tm, tiles_per_batch, nb):
    R = xr.shape[0]
    tn = 768
    mod_idx = lambda i, j: (jnp.minimum(i // tiles_per_batch, nb), 0, 0)
    return pl.pallas_call(
        _inproj_kernel,
        grid=(R // tm, PROJ_W // tn),
        in_specs=[pl.BlockSpec((tm, D), lambda i, j: (i, 0)),
                  pl.BlockSpec((1, D), lambda i, j: (0, 0)),
                  pl.BlockSpec((1, 1, D), mod_idx),
                  pl.BlockSpec((1, 1, D), mod_idx),
                  pl.BlockSpec((D, tn), lambda i, j: (0, j))],
        out_specs=pl.BlockSpec((tm, tn), lambda i, j: (i, j)),
        out_shape=jax.ShapeDtypeStruct((R, PROJ_W), BF),
        scratch_shapes=[pltpu.VMEM((tm, D), BF)],
        compiler_params=_cp(("arbitrary", "arbitrary")),
        name="inproj",
    )(xr, g, sh, sc, w)


PREP_TM = 256
QS_MLA = (MLA_NOPE + MLA_ROPE) ** -0.5 * LOG2E
QS_GQA = GQA_DH ** -0.5 * LOG2E


def _prep_kernel(cq_ref, ckv_ref, krz_ref, aq_ref, ak_ref, gq_ref, gkv_ref, wuq_ref, wk_ref, wv_ref,
                 gaq_ref, gak_ref, cg_ref, sg_ref, cm_ref, sm_ref,
                 qm_ref, km_ref, vm_ref, qg_ref, kg_ref):
    lane = lax.broadcasted_iota(I32, (PREP_TM, 128), 1)

    def rope(t, c, s, half):
        nxt = pltpu.roll(t, 128 - half, 1)
        prv = pltpu.roll(t, half, 1)
        return t * c + jnp.where((lane % (2 * half)) < half, nxt, prv) * s

    cm, sm = cm_ref[...], sm_ref[...]
    cg, sg = cg_ref[...], sg_ref[...]

    q = _dot(_rms(cq_ref[...].astype(F32), gq_ref[...]).astype(BF), wuq_ref[...])
    for h in range(MLA_H):
        a = 256 * h
        qm_ref[:, a:a + 128] = (q[:, a:a + 128] * QS_MLA).astype(BF)
        qm_ref[:, a + 128:a + 256] = (rope(q[:, a + 128:a + 256], cm, sm, MLA_ROPE // 4) * QS_MLA).astype(BF)

    n = _rms(ckv_ref[...].astype(F32), gkv_ref[...]).astype(BF)
    kn = _dot(n, wk_ref[...])
    vm_ref[...] = _dot(n, wv_ref[...]).astype(BF)
    kr = rope(krz_ref[...].astype(F32), cm, sm, MLA_ROPE // 4).astype(BF)
    for h in range(MLA_H):
        a = 256 * h
        km_ref[:, a:a + 128] = kn[:, 128 * h:128 * h + 128].astype(BF)
        km_ref[:, a + 128:a + 256] = kr

    for h in range(GQA_H):
        t = _rms(aq_ref[:, 128 * h:128 * h + 128].astype(F32), gaq_ref[...])
        qg_ref[:, 128 * h:128 * h + 128] = (rope(t, cg, sg, GQA_DH // 4) * QS_GQA).astype(BF)
    for h in range(GQA_KV):
        t = _rms(ak_ref[:, 128 * h:128 * h + 128].astype(F32), gak_ref[...])
        kg_ref[:, 128 * h:128 * h + 128] = rope(t, cg, sg, GQA_DH // 4).astype(BF)


def _rope_tables(S):
    pos = np.arange(S)
    rows, cols = pos // GRID_W, pos % GRID_W

    def tab(d, width):
        half = d // 2
        inv = ROPE_THETA ** (-np.arange(0, half, 2, dtype=np.float64) / half)
        ar, ac = rows[:, None] * inv[None], cols[:, None] * inv[None]
        c = np.concatenate([np.cos(ar), np.cos(ar), np.cos(ac), np.cos(ac)], 1)
        s = np.concatenate([-np.sin(ar), np.sin(ar), -np.sin(ac), np.sin(ac)], 1)
        ci = np.ones((CT, d)); si = np.zeros((CT, d))
        c = np.concatenate([c, ci], 0); s = np.concatenate([s, si], 0)
        pad = np.zeros((S + CT, width - d))
        return (jnp.asarray(np.concatenate([c, pad], 1), F32), jnp.asarray(np.concatenate([s, pad], 1), F32))

    return tab(GQA_DH, 128) + tab(MLA_ROPE, 128)


def _prep(proj, gq, gkv, wuq, wk, wv, gaq, gak, tabs, *, S, RL):
    R = proj.shape[0]
    tm = PREP_TM
    nlat = RL // tm
    spb = S // tm
    tab_idx = lambda i: (jnp.where(i < nlat, i % spb, spb), 0)
    col = lambda w, off: pl.BlockSpec((tm, w), lambda i: (i, off // w))
    full = lambda a: pl.BlockSpec(a.shape, lambda i: (0,) * a.ndim)
    tspec = pl.BlockSpec((tm, 128), tab_idx)
    widths = [MLA_H * 256, MLA_H * 256, MLA_H * MLA_DV, GQA_H * GQA_DH, GQA_KV * GQA_DH]
    return pl.pallas_call(
        _prep_kernel,
        grid=(R // tm,),
        in_specs=[col(MLA_QR, OFF_CQ), col(MLA_KVR, OFF_CKV), col(128, OFF_KRZ), col(768, OFF_AQ), col(256, OFF_AK),
                  full(gq), full(gkv), full(wuq), full(wk), full(wv), full(gaq), full(gak),
                  tspec, tspec, tspec, tspec],
        out_specs=[pl.BlockSpec((tm, w), lambda i: (i, 0)) for w in widths],
        out_shape=[jax.ShapeDtypeStruct((R, w), BF) for w in widths],
        compiler_params=_cp(("arbitrary",)),
        name="qk_prep",
    )(proj, proj, proj, proj, proj, gq, gkv, wuq, wk, wv, gaq, gak, *tabs)


GLA_TM = 256


def _gla_kernel(qf_ref, kf_ref, vf_ref, zf_ref, qb_ref, kb_ref, vb_ref, zb_ref, wg_ref, bg_ref, tri_ref,
                of_ref, ob_ref, st_scr, la_scr):
    @pl.when(pl.program_id(1) == 0)
    def _():
        st_scr[...] = jnp.zeros_like(st_scr)

    dirs = ((qf_ref, kf_ref, vf_ref, zf_ref, of_ref), (qb_ref, kb_ref, vb_ref, zb_ref, ob_ref))
    for d, (_, _, _, z_ref, _) in enumerate(dirs):
        z = _dot(z_ref[...], wg_ref[d]) + bg_ref[d]
        la_scr[d] = (jnp.minimum(z, 0.0) - jnp.log(1.0 + jnp.exp(-jnp.abs(z)))) * (1.0 / GLA_TAU)
    C = GLA_CHUNK
    nc = GLA_TM // C
    for cc in range(nc):
        for d, (q_ref, k_ref, v_ref, _, o_ref) in enumerate(dirs):
            tri = tri_ref[d]
            keep = tri > 0
            r0 = (cc if d == 0 else nc - 1 - cc) * C
            la = la_scr[d, r0:r0 + C, :]
            h1 = la.astype(BF)
            r1 = la - h1.astype(F32)
            h2 = r1.astype(BF)
            h3 = (r1 - h2.astype(F32)).astype(BF)
            b = _dot(tri, h1) + _dot(tri, h2) + _dot(tri, h3)
            tot = jnp.sum(la, axis=0, keepdims=True)
            q = q_ref[r0:r0 + C, :].astype(F32)
            k = k_ref[r0:r0 + C, :].astype(F32)
            qd = (q * (GLA_DK ** -0.5) * jnp.exp(b)).astype(BF)
            ki = (k * jnp.exp(-b)).astype(BF)
            ke = (k * jnp.exp(tot - b)).astype(BF)
            dec = jnp.exp(tot)
            v = v_ref[r0:r0 + C, :]
            for h in range(GLA_H):
                sl = slice(GLA_DK * h, GLA_DK * (h + 1))
                sv = slice(GLA_DV * h, GLA_DV * (h + 1))
                att = jnp.where(keep, _dot_nt(qd[:, sl], ki[:, sl]), 0.0).astype(BF)
                st = st_scr[d, h]
                o_ref[r0:r0 + C, sv] = _dot(att, v[:, sv]) + _dot_nt(qd[:, sl], st.astype(BF))
                st_scr[d, h] = st * dec[:, sl] + _dot_tn(v[:, sv], ke[:, sl])


def _gla(proj, wg, bg, tri, *, B, S, RL):
    R = proj.shape[0]
    tm = GLA_TM
    nj = S // tm + 1
    lat0 = lambda b: b * (S // tm)

    def rb(d):
        def f(b, j):
            jl = (j - 1) if d == 0 else (S // tm - j)
            return jnp.where(j == 0, RL // tm + b, lat0(b) + jl)
        return f

    col = lambda d, w, off: pl.BlockSpec((tm, w), lambda b, j: (rb(d)(b, j), off // w))
    full = lambda a: pl.BlockSpec(a.shape, lambda b, j: (0,) * a.ndim)
    ins = lambda d: [col(d, 256, OFF_GQ), col(d, 256, OFF_GK), col(d, 512, OFF_GV), col(d, 128, OFF_KRZ)]
    out = lambda d: pl.BlockSpec((tm, GLA_H * GLA_DV), lambda b, j: (rb(d)(b, j), 0))
    return pl.pallas_call(
        _gla_kernel,
        grid=(B, nj),
        in_specs=ins(0) + ins(1) + [full(wg), full(bg), full(tri)],
        out_specs=[out(0), out(1)],
        out_shape=[jax.ShapeDtypeStruct((R, GLA_H * GLA_DV), F32)] * 2,
        scratch_shapes=[pltpu.VMEM((2, GLA_H, GLA_DV, GLA_DK), F32), pltpu.VMEM((2, tm, GLA_H * GLA_DK), F32)],
        compiler_params=_cp(("arbitrary", "arbitrary")),
        name="gla",
    )(*([proj] * 8), wg, bg, tri)


def _flash_kernel(*refs, nseg, seg_lens, tq, dv, ck):
    q_ref = refs[0]
    k_refs = refs[1:1 + nseg]
    v_refs = refs[1 + nseg:1 + 2 * nseg]
    o_ref = refs[1 + 2 * nseg]
    s_scr = refs[2 + 2 * nseg]
    q = q_ref[...]
    m = jnp.full((tq, 128), -jnp.inf, F32)
    chunks = []
    off = 0
    for si in range(nseg):
        for c0 in range(0, seg_lens[si], ck):
            cl = min(ck, seg_lens[si] - c0)
            s = _dot_nt(q, k_refs[si][c0:c0 + cl, :])
            s_scr[:, off:off + cl] = s
            for l0 in range(0, cl, 128):
                m = jnp.maximum(m, s[:, l0:l0 + 128])
            chunks.append((si, c0, cl, off))
            off += cl
    mrow = jnp.max(m, axis=1, keepdims=True)
    l = jnp.zeros((tq, 128), F32)
    acc = jnp.zeros((tq, dv), F32)
    for si, c0, cl, off in chunks:
        p = jnp.exp2(s_scr[:, off:off + cl] - mrow)
        for l0 in range(0, cl, 128):
            l = l + p[:, l0:l0 + 128]
        acc = acc + _dot(p.astype(BF), v_refs[si][c0:c0 + cl, :])
    o_ref[...] = (acc / jnp.sum(l, axis=1, keepdims=True)).astype(o_ref.dtype)


def _attention(qa, ka, va, *, B, S, RL, hq, group, dk, dv, vbase, ctx_queries, name):
    cb = RL // CT
    kh = lambda h: h // group
    if ctx_queries:
        tq, nq, n_out = CT, 1, B * CT
        q_spec = pl.BlockSpec((tq, dk), lambda b, h, i: (cb + b, h))
        o_spec = pl.BlockSpec((tq, dv), lambda b, h, i: (b, h))
        k_specs = [pl.BlockSpec((CT, dk), lambda b, h, i: (cb + b, kh(h)))]
        v_specs = [pl.BlockSpec((CT, dv), lambda b, h, i: (cb + b, vbase + kh(h)))]
        seg_lens = (CT,)
    else:
        tq, nq, n_out = 256, S // 256, RL
        q_spec = pl.BlockSpec((tq, dk), lambda b, h, i: (b * nq + i, h))
        o_spec = pl.BlockSpec((tq, dv), lambda b, h, i: (b * nq + i, h))
        k_specs = [pl.BlockSpec((CT, dk), lambda b, h, i: (cb + b, kh(h))),
                   pl.BlockSpec((S, dk), lambda b, h, i: (b, kh(h)))]
        v_specs = [pl.BlockSpec((CT, dv), lambda b, h, i: (cb + b, vbase + kh(h))),
                   pl.BlockSpec((S, dv), lambda b, h, i: (b, vbase + kh(h)))]
        seg_lens = (CT, S)
    nseg = len(seg_lens)
    kern = functools.partial(_flash_kernel, nseg=nseg, seg_lens=seg_lens, tq=tq, dv=dv, ck=512)
    return pl.pallas_call(
        kern,
        grid=(B, hq, nq),
        in_specs=[q_spec] + k_specs + v_specs,
        out_specs=o_spec,
        out_shape=jax.ShapeDtypeStruct((n_out, hq * dv), BF),
        scratch_shapes=[pltpu.VMEM((tq, sum(seg_lens)), F32)],
        compiler_params=_cp(("arbitrary", "arbitrary", "arbitrary")),
        name=name,
    )(qa, *([ka] * nseg), *([va] * nseg))


WOUT_TM = 256


def _route(logit, rb):
    T = logit.shape[1]
    gsz = N_EXP // N_GRP
    scores = _sigmoid(logit)
    sel = scores + rb
    ninf = jnp.float32(-jnp.inf)
    sel3 = sel.reshape(N_GRP, gsz, T)
    idx3 = lax.broadcasted_iota(I32, (N_GRP, gsz, T), 1)
    m1 = jnp.max(sel3, axis=1, keepdims=True)
    first = jnp.min(jnp.where(sel3 == m1, idx3, gsz), axis=1, keepdims=True)
    m2 = jnp.max(jnp.where(idx3 == first, ninf, sel3), axis=1, keepdims=True)
    gs = (m1 + m2).reshape(N_GRP, T)
    gi = lax.broadcasted_iota(I32, (N_GRP, T), 0)
    grank = jnp.zeros((N_GRP, T), I32)
    for j in range(N_GRP):
        row = gs[j:j + 1, :]
        grank = grank + ((row > gs) | ((row == gs) & (j < gi))).astype(I32)
    gkeep = (grank < TOPK_GRP).reshape(N_GRP, 1, T)
    selm = jnp.where(jnp.broadcast_to(gkeep, (N_GRP, gsz, T)), sel3, ninf).reshape(N_EXP, T)
    ei = lax.broadcasted_iota(I32, (N_EXP, T), 0)
    erank = jnp.zeros((N_EXP, T), I32)
    for j in range(N_EXP):
        row = selm[j:j + 1, :]
        erank = erank + ((row > selm) | ((row == selm) & (j < ei))).astype(I32)
    chosen = erank < TOP_K
    w = jnp.where(chosen, scores, 0.0)
    wd = w / jnp.sum(w, axis=0, keepdims=True) * ROUTED_SCALE
    return wd, chosen


def _store_tile_rows(ref, packed, tm):
    for cblk in range(HALF // 128):
        ref[pl.ds(cblk, tm, stride=8), :] = packed[:, 128 * cblk:128 * (cblk + 1)]


def _load_tile_rows(ref, tm):
    los, his = [], []
    for cblk in range(HALF // 128):
        lo, hi = _unpack_halves(ref[pl.ds(cblk, tm, stride=8), :])
        los.append(lo.astype(BF))
        his.append(hi.astype(BF))
    return jnp.concatenate(los, axis=1), jnp.concatenate(his, axis=1)


def _wout_kernel(of_ref, ob_ref, gr_ref, mla_ref, gqa_ref, x_ref, w_ref, gout_ref, gpm_ref, gtm_ref,
                 gpf_ref, shf_ref, scf_ref, rwt_ref, rb_ref, ust_ref, lst_ref,
                 xn_ref, hfp_ref, r8_ref, e8_ref, w8_ref, cnt_ref, cnt_scr):
    @pl.when(pl.program_id(0) == 0)
    def _():
        cnt_scr[...] = jnp.zeros_like(cnt_scr)

    og = of_ref[...] + ob_ref[...]
    gr = gr_ref[...].astype(F32)
    parts = []
    for h in range(GLA_H):
        sv = slice(GLA_DV * h, GLA_DV * (h + 1))
        g = gr[:, sv]
        parts.append((_rms(og[:, sv], gout_ref[...]) * (g * _sigmoid(g))).astype(BF))
    gla = jnp.concatenate(parts, axis=1)
    n_g, n_m = GLA_H * GLA_DV, MLA_H * MLA_DV
    y = (_dot(gla, w_ref[0:n_g, :]) + _dot(mla_ref[...], w_ref[n_g:n_g + n_m, :])
         + _dot(gqa_ref[...], w_ref[n_g + n_m:, :]))
    xn = x_ref[...] + gtm_ref[0] * _rms(y, gpm_ref[...])
    xn_ref[...] = xn
    hf = _rms(xn, gpf_ref[...]) * (1.0 + scf_ref[0]) + shf_ref[0]
    _store_tile_rows(hfp_ref, _pack_halves(hf), WOUT_TM)
    hh, hl = _split2(hf)
    rh, rl = _split2(rwt_ref[...])
    logit = _dot_nt(rh, hh) + _dot_nt(rl, hh) + _dot_nt(rh, hl)
    wd, chosen = _route(logit, rb_ref[...])

    cb = jnp.where(chosen, 1.0, 0.0)
    before = _dot(cb.astype(BF), ust_ref[...])
    rank = cnt_scr[:, 0:1] + before
    slot = _dot(lst_ref[...], cb.astype(BF))
    eid = lax.broadcasted_iota(I32, chosen.shape, 0).astype(F32)
    r8, e8, w8 = [], [], []
    for k in range(TOP_K):
        pick = chosen & (slot == float(k))
        r8.append(jnp.sum(jnp.where(pick, rank, 0.0), axis=0, keepdims=True))
        e8.append(jnp.sum(jnp.where(pick, eid, 0.0), axis=0, keepdims=True))
        w8.append(jnp.sum(jnp.where(pick, wd, 0.0), axis=0, keepdims=True))
    r8_ref[...] = jnp.concatenate(r8, axis=0).astype(I32)
    e8_ref[...] = jnp.concatenate(e8, axis=0).astype(I32)
    w8_ref[...] = jnp.concatenate(w8, axis=0)
    cnt_scr[...] = cnt_scr[...] + jnp.sum(cb, axis=1, keepdims=True)
    cnt_ref[...] = cnt_scr[...].astype(I32)


def _wout(gla_o, proj, mla_o, gqa_o, xr, w, gout, gpm, gtm, gpf, shf, scf, rwt, rb, *, n_rows, tiles_per_batch, nb):
    tm = WOUT_TM
    mod_idx = lambda i: (jnp.minimum(i // tiles_per_batch, nb), 0, 0)
    row = lambda w_: pl.BlockSpec((tm, w_), lambda i: (i, 0))
    full = lambda a: pl.BlockSpec(a.shape, lambda i: (0,) * a.ndim)
    mspec = pl.BlockSpec((1, 1, D), mod_idx)
    it = np.arange(tm)
    ust = jnp.asarray(it[:, None] < it[None, :], BF)
    ie = np.arange(N_EXP)
    lst = jnp.asarray(ie[None, :] < ie[:, None], BF)
    k8 = pl.BlockSpec((TOP_K, tm), lambda i: (0, i))
    return pl.pallas_call(
        _wout_kernel,
        grid=(n_rows // tm,),
        in_specs=[row(512), row(512),
                  pl.BlockSpec((tm, 512), lambda i: (i, OFF_GR // 512)),
                  row(768), row(768), row(D), full(w), full(gout), full(gpm), mspec,
                  full(gpf), mspec, mspec, full(rwt), full(rb), full(ust), full(lst)],
        out_specs=[row(D), pl.BlockSpec((tm * 8, 128), lambda i: (i, 0)), k8, k8, k8,
                   pl.BlockSpec((N_EXP, 128), lambda i: (0, 0))],
        out_shape=[jax.ShapeDtypeStruct((n_rows, D), F32), jax.ShapeDtypeStruct((n_rows * 8, 128), PACKED),
                   jax.ShapeDtypeStruct((TOP_K, n_rows), I32), jax.ShapeDtypeStruct((TOP_K, n_rows), I32),
                   jax.ShapeDtypeStruct((TOP_K, n_rows), F32), jax.ShapeDtypeStruct((N_EXP, 128), I32)],
        scratch_shapes=[pltpu.VMEM((N_EXP, 128), F32)],
        compiler_params=_cp(("arbitrary",)),
        name="wout_route",
    )(gla_o[0], gla_o[1], proj, mla_o, gqa_o, xr, w, gout, gpm, gtm, gpf, shf, scf, rwt, rb, ust, lst)


MOE_TM = 512
DISP_TM = 256
COMB_TM = 128


def _tile_row(ref, r):
    return ref.at[pl.ds(pl.multiple_of(r * 8, 8), 8), :]


def _dispatch_kernel(cnt_ref, pst_ref, ntl_ref, nu_ref, pos_ref, hfp_ref, xs_ref, zero_scr, sem, zsem):
    i = pl.program_id(0)
    tm = DISP_TM
    tile_rows = MOE_TM * 8
    n_tiles = xs_ref.shape[0] // tile_rows

    def issue(t, carry):
        for k in range(TOP_K):
            p = pos_ref[0, 0, k * tm + t]
            pltpu.make_async_copy(_tile_row(hfp_ref, t), _tile_row(xs_ref, p), sem).start(priority=k % 2)
        return carry

    lax.fori_loop(0, tm, issue, 0, unroll=2)

    @pl.when(i == 0)
    def _():
        zero_scr[...] = jnp.zeros_like(zero_scr)

        def per_expert(e, carry):
            cnt = cnt_ref[e]
            padlen = ntl_ref[e] * MOE_TM - cnt
            off = pst_ref[e] + cnt

            pieces = []
            bit = MOE_TM // 2
            while bit >= 1:
                take = padlen & bit
                dst = xs_ref.at[pl.ds(pl.multiple_of(off * 8, 8), bit * 8), :]
                pieces.append((take != 0, pltpu.make_async_copy(zero_scr.at[pl.ds(0, bit * 8), :], dst, zsem)))
                off = off + take
                bit //= 2
            for go, cp in pieces:
                pl.when(go)(cp.start)
            for go, cp in pieces:
                pl.when(go)(cp.wait)
            return carry

        lax.fori_loop(0, N_EXP, per_expert, 0)

        def ztile(t):
            dst = xs_ref.at[pl.ds(pl.multiple_of(t * tile_rows, tile_rows), tile_rows), :]
            return pltpu.make_async_copy(zero_scr, dst, zsem)

        lax.fori_loop(nu_ref[0], n_tiles, lambda t, cr: (ztile(t).start(), cr)[1], 0)
        lax.fori_loop(nu_ref[0], n_tiles, lambda t, cr: (ztile(t).wait(), cr)[1], 0)

    for k in range(TOP_K):
        pltpu.make_async_copy(hfp_ref, xs_ref.at[pl.ds(0, tm * 8), :], sem).wait()


def _dispatch(cnt, pst, ntl, n_used, pos_t, hfp, *, n_rows_sorted):
    n = hfp.shape[0] // 8
    tm = DISP_TM
    grid_spec = pltpu.PrefetchScalarGridSpec(
        num_scalar_prefetch=4,
        grid=(n // tm,),
        in_specs=[pl.BlockSpec((1, 1, tm * TOP_K), lambda i, *_: (i, 0, 0), memory_space=pltpu.SMEM),
                  pl.BlockSpec((tm * 8, 128), lambda i, *_: (i, 0))],
        out_specs=pl.BlockSpec(memory_space=pl.ANY),
        scratch_shapes=[pltpu.VMEM((MOE_TM * 8, 128), PACKED), pltpu.SemaphoreType.DMA, pltpu.SemaphoreType.DMA],
    )
    return pl.pallas_call(
        _dispatch_kernel,
        grid_spec=grid_spec,
        out_shape=jax.ShapeDtypeStruct((n_rows_sorted * 8, 128), PACKED),
        compiler_params=_cp(("arbitrary",)),
        name="moe_dispatch",
    )(cnt, pst, ntl, n_used, pos_t, hfp)


def _gmm_kernel(te_ref, nu_ref, x_ref, w1_ref, w3_ref, w2_ref, y_ref, w1b, w3b, w2b):
    i = pl.program_id(0)
    used = i < nu_ref[0]

    @pl.when(used)
    def _():
        changed = jnp.logical_or(i == 0, te_ref[jnp.maximum(i - 1, 0)] != te_ref[i])

        @pl.when(changed)
        def _():
            w1b[...] = w1_ref[0].astype(BF)
            w3b[...] = w3_ref[0].astype(BF)
            w2b[...] = w2_ref[0].astype(BF)

        xa, xb = _load_tile_rows(x_ref, MOE_TM)
        h1 = _dot(xa, w1b[0:HALF, :]) + _dot(xb, w1b[HALF:, :])
        h3 = _dot(xa, w3b[0:HALF, :]) + _dot(xb, w3b[HALF:, :])
        h = (h1 * _sigmoid(h1) * h3).astype(BF)
        _store_tile_rows(y_ref, _pack_halves(_dot(h, w2b[...])), MOE_TM)

    @pl.when(jnp.logical_not(used))
    def _():
        y_ref[...] = jnp.zeros_like(y_ref)


def _gmm(tile_e, n_used, xs, w1, w3, w2, layer):
    tm = MOE_TM
    n_tiles = xs.shape[0] // (tm * 8)
    last = lambda i, nu: jnp.minimum(i, nu[0] - 1)
    wsel = lambda i, te, nu: (layer * N_EXP + te[last(i, nu)], 0, 0)
    grid_spec = pltpu.PrefetchScalarGridSpec(
        num_scalar_prefetch=2,
        grid=(n_tiles,),
        in_specs=[pl.BlockSpec((tm * 8, 128), lambda i, te, nu: (last(i, nu), 0)),
                  pl.BlockSpec((1, D, D_EXP), wsel),
                  pl.BlockSpec((1, D, D_EXP), wsel),
                  pl.BlockSpec((1, D_EXP, D), wsel)],
        out_specs=pl.BlockSpec((tm * 8, 128), lambda i, te, nu: (i, 0)),
        scratch_shapes=[pltpu.VMEM((D, D_EXP), BF), pltpu.VMEM((D, D_EXP), BF), pltpu.VMEM((D_EXP, D), BF)],
    )
    return pl.pallas_call(
        _gmm_kernel,
        grid_spec=grid_spec,
        out_shape=jax.ShapeDtypeStruct(xs.shape, PACKED),
        compiler_params=_cp(("arbitrary",)),
        name="moe_experts",
    )(tile_e, n_used, xs, w1, w3, w2)


def _combine_kernel(pos_ref, w8_ref, hfp_ref, x_ref, s1_ref, s3_ref, s2_ref, gpost_ref, gtf_ref, ys_ref,
                    o_ref, ybuf, sem):
    tm = COMB_TM

    def issue(t, carry):
        for k in range(TOP_K):
            p = pos_ref[0, 0, k * tm + t]
            pltpu.make_async_copy(_tile_row(ys_ref, p), _tile_row(ybuf.at[k], t), sem).start(priority=k % 2)
        return carry

    lax.fori_loop(0, tm, issue, 0, unroll=2)

    xa, xb = _load_tile_rows(hfp_ref, tm)
    h1 = _dot(xa, s1_ref[0:HALF, :]) + _dot(xb, s1_ref[HALF:, :])
    h3 = _dot(xa, s3_ref[0:HALF, :]) + _dot(xb, s3_ref[HALF:, :])
    sh = _dot((h1 * _sigmoid(h1) * h3).astype(BF), s2_ref[...])

    for k in range(TOP_K):
        pltpu.make_async_copy(ys_ref.at[pl.ds(0, tm * 8), :], ybuf.at[k], sem).wait()

    w8 = w8_ref[...]
    wk = [w8[:, k:k + 1] for k in range(TOP_K)]
    los, his = [], []
    for cblk in range(HALF // 128):
        a_lo = sh[:, 128 * cblk:128 * (cblk + 1)]
        a_hi = sh[:, HALF + 128 * cblk:HALF + 128 * (cblk + 1)]
        for k in range(TOP_K):
            lo, hi = _unpack_halves(ybuf[k, pl.ds(cblk, tm, stride=8), :])
            a_lo = a_lo + wk[k] * lo
            a_hi = a_hi + wk[k] * hi
        los.append(a_lo)
        his.append(a_hi)
    yf = jnp.concatenate(los + his, axis=1)
    o_ref[...] = x_ref[...] + gtf_ref[0] * _rms(yf, gpost_ref[...])


def _combine(pos_t, w8, hfp, xn, s1, s3, s2, gpost, gtf, ys, *, tiles_per_batch, nb):
    n = hfp.shape[0] // 8
    tm = COMB_TM
    mod_idx = lambda i: (jnp.minimum(i // tiles_per_batch, nb), 0, 0)
    full = lambda a: pl.BlockSpec(a.shape, lambda i: (0,) * a.ndim)
    return pl.pallas_call(
        _combine_kernel,
        grid=(n // tm,),
        in_specs=[pl.BlockSpec((1, 1, tm * TOP_K), lambda i: (i, 0, 0), memory_space=pltpu.SMEM),
                  pl.BlockSpec((tm, TOP_K), lambda i: (i, 0)),
                  pl.BlockSpec((tm * 8, 128), lambda i: (i, 0)),
                  pl.BlockSpec((tm, D), lambda i: (i, 0)),
                  full(s1), full(s3), full(s2), full(gpost),
                  pl.BlockSpec((1, 1, D), mod_idx),
                  pl.BlockSpec(memory_space=pl.ANY)],
        out_specs=pl.BlockSpec((tm, D), lambda i: (i, 0)),
        out_shape=jax.ShapeDtypeStruct((n, D), F32),
        scratch_shapes=[pltpu.VMEM((TOP_K, tm * 8, 128), PACKED), pltpu.SemaphoreType.DMA],
        compiler_params=_cp(("arbitrary",)),
        name="moe_combine",
    )(pos_t, w8, hfp, xn, s1, s3, s2, gpost, gtf, ys)


def _dispatch_tables(counts, r8, e8):
    n = r8.shape[1]
    ntl = (counts + MOE_TM - 1) // MOE_TM
    tend = jnp.cumsum(ntl)
    pst = (tend - ntl) * MOE_TM
    onehot = e8[None] == jnp.arange(N_EXP, dtype=I32)[:, None, None]
    pos = r8 + jnp.sum(jnp.where(onehot, pst[:, None, None], 0), axis=0)
    n_tiles = -(-(n * TOP_K) // MOE_TM) + N_EXP
    tile_e = jnp.sum((jnp.arange(n_tiles, dtype=I32)[:, None] >= tend[None, :]).astype(I32), axis=1)
    tile_e = jnp.minimum(tile_e, N_EXP - 1)
    return pst.astype(I32), ntl.astype(I32), pos.astype(I32), tile_e.astype(I32), tend[-1:].astype(I32), n_tiles


def _pos_blocks(pos, tm):
    k, n = pos.shape
    return pos.reshape(k, n // tm, tm).transpose(1, 0, 2).reshape(n // tm, 1, k * tm)


def _inproj_weight(w_in):
    offs = np.cumsum((0, 256, 256, 512, 512, 16, 16, 384, 512, 64, 768, 256, 256))
    p = [w_in[:, offs[i]:offs[i + 1]] for i in range(12)]
    gq, gk, gv, gr, gzf, gzb, cq, ckv, kr, aq, ak, av = p
    krz = jnp.concatenate([kr, gzf, gzb, jnp.zeros((D, 32), w_in.dtype)], axis=1)
    return jnp.concatenate([gv, gr, ckv, aq, cq, krz, gq, gk, ak, av], axis=1).astype(BF)


def _mla_weights(w_uq, w_ukv):
    wq = w_uq.reshape(MLA_QR, MLA_H, MLA_NOPE + MLA_ROPE)
    wq = jnp.concatenate([wq, jnp.zeros((MLA_QR, MLA_H, 256 - MLA_NOPE - MLA_ROPE), wq.dtype)], axis=2)
    wkv = w_ukv.reshape(MLA_KVR, MLA_H, MLA_NOPE + MLA_DV)
    wk = wkv[:, :, :MLA_NOPE].reshape(MLA_KVR, MLA_H * MLA_NOPE)
    wv = wkv[:, :, MLA_NOPE:].reshape(MLA_KVR, MLA_H * MLA_DV)
    return wq.reshape(MLA_QR, MLA_H * 256).astype(BF), wk.astype(BF), wv.astype(BF)


def _gla_gate_weights(wf, bf, wb, bb):
    z = jnp.zeros((128, GLA_H * GLA_DK), F32)
    wg = jnp.stack([z.at[64:80].set(wf), z.at[80:96].set(wb)]).astype(BF)
    bg = jnp.stack([bf, bb]).reshape(2, 1, GLA_H * GLA_DK)
    return wg, bg


def kernel(x, c, ctx, c_ctx, w_mod, b_mod, g_pre_mix, g_post_mix, g_pre_ffn, g_post_ffn, w_in, gla_wg2_f, gla_bg_f, gla_wg2_b, gla_bg_b, gla_g_out, mla_g_q, mla_w_uq, mla_g_kv, mla_w_ukv, gqa_g_q, gqa_g_k, w_out, router_w, router_b, exp_w1, exp_w3, exp_w2, sh_w1, sh_w3, sh_w2):
    B, S, _ = x.shape
    depth = w_mod.shape[0]
    assert ctx.shape[1] == CT and S % 256 == 0 and B < 8
    RL, RC = B * S, B * CT
    R = RL + RC

    xr = jnp.concatenate([x.reshape(RL, D), ctx.reshape(RC, D)], axis=0)
    cvecs = jnp.concatenate([c, c_ctx[None], jnp.zeros((8 - B - 1, D), F32)], axis=0)
    mods = _modulation(cvecs, w_mod, b_mod)
    tabs = _rope_tables(S)
    ii = np.arange(GLA_CHUNK)
    tri = jnp.asarray(np.stack([ii[:, None] >= ii[None, :], ii[:, None] <= ii[None, :]]), BF)

    in_tm = 1024 if (S % 1024 == 0 and RC % 1024 == 0) else 256
    row2 = lambda v: v.reshape(1, -1)
    ew1 = exp_w1.reshape(depth * N_EXP, D, D_EXP)
    ew3 = exp_w3.reshape(depth * N_EXP, D, D_EXP)
    ew2 = exp_w2.reshape(depth * N_EXP, D_EXP, D)

    for l in range(depth):
        want_ctx = l < depth - 1
        m = mods[l].reshape(8, 6, 1, D)
        sh_m, sc_m, gt_m, sh_f, sc_f, gt_f = [m[:, i] for i in range(6)]

        proj = _inproj(xr, row2(g_pre_mix[l]), sh_m, sc_m, _inproj_weight(w_in[l]),
                       tm=in_tm, tiles_per_batch=S // in_tm, nb=B)
        wuq, wk, wv = _mla_weights(mla_w_uq[l], mla_w_ukv[l])
        qm, km, vm, qg, kg = _prep(proj, row2(mla_g_q[l]), row2(mla_g_kv[l]), wuq, wk, wv,
                                   row2(gqa_g_q[l]), row2(gqa_g_k[l]), tabs, S=S, RL=RL)
        wg, bg = _gla_gate_weights(gla_wg2_f[l], gla_bg_f[l], gla_wg2_b[l], gla_bg_b[l])
        gla_o = _gla(proj, wg, bg, tri, B=B, S=S, RL=RL)

        att = functools.partial(_attention, B=B, S=S, RL=RL)
        mla = functools.partial(att, qm, km, vm, hq=MLA_H, group=1, dk=256, dv=MLA_DV, vbase=0)
        gqa = functools.partial(att, qg, kg, proj, hq=GQA_H, group=GQA_H // GQA_KV, dk=GQA_DH, dv=GQA_DH,
                                vbase=OFF_AV // GQA_DH)
        mla_o = mla(ctx_queries=False, name="mla_lat")
        gqa_o = gqa(ctx_queries=False, name="gqa_lat")
        if want_ctx:
            mla_o = jnp.concatenate([mla_o, mla(ctx_queries=True, name="mla_ctx")], axis=0)
            gqa_o = jnp.concatenate([gqa_o, gqa(ctx_queries=True, name="gqa_ctx")], axis=0)
        n_rows = R if want_ctx else RL

        xn, hfp, r8, e8, w8, cnt = _wout(gla_o, proj, mla_o, gqa_o, xr, w_out[l].astype(BF), row2(gla_g_out[l]),
                                         row2(g_post_mix[l]), gt_m, row2(g_pre_ffn[l]), sh_f, sc_f,
                                         router_w[l].T, router_b[l].reshape(N_EXP, 1),
                                         n_rows=n_rows, tiles_per_batch=S // WOUT_TM, nb=B)

        counts = cnt[:, 0]
        pst, ntl, pos, tile_e, n_used, n_tiles = _dispatch_tables(counts, r8, e8)
        xs = _dispatch(counts, pst, ntl, n_used, _pos_blocks(pos, DISP_TM), hfp, n_rows_sorted=n_tiles * MOE_TM)
        ys = _gmm(tile_e, n_used, xs, ew1, ew3, ew2, l)
        xo = _combine(_pos_blocks(pos, COMB_TM), w8.T, hfp, xn,
                      sh_w1[l].astype(BF), sh_w3[l].astype(BF), sh_w2[l].astype(BF),
                      row2(g_post_ffn[l]), gt_f, ys, tiles_per_batch=S // COMB_TM, nb=B)
        xr = xo
    return xr[:RL].reshape(B, S, D)
```

```python
import functools
import math

import numpy as np
import jax
import jax.numpy as jnp
from jax import lax
from jax.experimental import pallas as pl
from jax.experimental.pallas import tpu as pltpu

F32 = jnp.float32
BF = jnp.bfloat16
PACKED = jnp.uint32
I32 = jnp.int32

D = 2048
GRID_W = 64
CT = 256
ROPE_THETA = 10000.0
EPS = 1e-6
GLA_H, GLA_DK, GLA_DV, GLA_RANK, GLA_TAU, GLA_CHUNK = 4, 64, 128, 16, 16.0, 64
MLA_H, MLA_QR, MLA_KVR, MLA_NOPE, MLA_ROPE, MLA_DV = 6, 384, 512, 128, 64, 128
GQA_H, GQA_KV, GQA_DH = 6, 2, 128
N_EXP, TOP_K, N_GRP, TOPK_GRP, D_EXP, D_SH = 64, 8, 8, 4, 512, 512
ROUTED_SCALE = 2.5
LOG2E = math.log2(math.e)
HALF = D // 2

OFF_GV, OFF_GR, OFF_CKV, OFF_AQ, OFF_CQ, OFF_KRZ, OFF_GQ, OFF_GK, OFF_AK, OFF_AV = (
    0, 512, 1024, 1536, 2304, 2688, 2816, 3072, 3328, 3584)
PROJ_W = 3840
VMEM_LIMIT = 56 * 1024 * 1024


def _cp(sem, vmem=None):
    return pltpu.CompilerParams(dimension_semantics=sem, vmem_limit_bytes=vmem or VMEM_LIMIT)


def _dot(a, b):
    return jnp.dot(a, b, preferred_element_type=F32)


def _dot_nt(a, b):
    return lax.dot_general(a, b, (((1,), (1,)), ((), ())), preferred_element_type=F32)


def _dot_tn(a, b):
    return lax.dot_general(a, b, (((0,), (0,)), ((), ())), preferred_element_type=F32)


def _sigmoid(x):
    return 1.0 / (1.0 + jnp.exp(-x))


def _rms(x, g):
    return x * lax.rsqrt(jnp.mean(x * x, axis=-1, keepdims=True) + EPS) * g


def _split2(x):
    hi = x.astype(BF)
    return hi, (x - hi.astype(F32)).astype(BF)


def _pack_halves(y):
    return pltpu.pack_elementwise([y[:, :HALF], y[:, HALF:]], packed_dtype=BF)


def _packed_zeros(shape):
    z = jnp.zeros(shape, F32)
    return pltpu.pack_elementwise([z, z], packed_dtype=BF)


def _unpack_halves(w):
    lo = pltpu.unpack_elementwise(w, index=0, packed_dtype=BF, unpacked_dtype=F32)
    hi = pltpu.unpack_elementwise(w, index=1, packed_dtype=BF, unpacked_dtype=F32)
    return lo, hi


def _mod_kernel(a_ref, w_ref, b_ref, o_ref):
    a = a_ref[...]
    a = a * _sigmoid(a)
    ah, al = _split2(a)
    wh, wl = _split2(w_ref[0])
    o_ref[0] = _dot(ah, wh) + _dot(al, wh) + _dot(ah, wl) + b_ref[0]


def _modulation(cvecs, w_mod, b_mod):
    L, _, n = w_mod.shape
    tn = 1024
    return pl.pallas_call(
        _mod_kernel,
        grid=(L, n // tn),
        in_specs=[pl.BlockSpec((8, D), lambda l, j: (0, 0)),
                  pl.BlockSpec((1, D, tn), lambda l, j: (l, 0, j)),
                  pl.BlockSpec((1, 1, tn), lambda l, j: (l, 0, j))],
        out_specs=pl.BlockSpec((1, 8, tn), lambda l, j: (l, 0, j)),
        out_shape=jax.ShapeDtypeStruct((L, 8, n), F32),
        compiler_params=_cp(("arbitrary", "arbitrary")),
        name="modulation",
    )(cvecs, w_mod, b_mod.reshape(L, 1, n))


def _inproj_kernel(x_ref, g_ref, sh_ref, sc_ref, w_ref, o_ref, a_scr):
    @pl.when(pl.program_id(1) == 0)
    def _():
        y = _rms(x_ref[...], g_ref[...])
        a_scr[...] = (y * (1.0 + sc_ref[0]) + sh_ref[0]).astype(BF)

    o_ref[...] = _dot(a_scr[...], w_ref[...]).astype(o_ref.dtype)


def _inproj(xr, g, sh, sc, w, *, tm, tiles_per_batch, nb):
    R = xr.shape[0]
    tn = 768
    mod_idx = lambda i, j: (jnp.minimum(i // tiles_per_batch, nb), 0, 0)
    return pl.pallas_call(
        _inproj_kernel,
        grid=(R // tm, PROJ_W // tn),
        in_specs=[pl.BlockSpec((tm, D), lambda i, j: (i, 0)),
                  pl.BlockSpec((1, D), lambda i, j: (0, 0)),
                  pl.BlockSpec((1, 1, D), mod_idx),
                  pl.BlockSpec((1, 1, D), mod_idx),
                  pl.BlockSpec((D, tn), lambda i, j: (0, j))],
        out_specs=pl.BlockSpec((tm, tn), lambda i, j: (i, j)),
        out_shape=jax.ShapeDtypeStruct((R, PROJ_W), BF),
        scratch_shapes=[pltpu.VMEM((tm, D), BF)],
        compiler_params=_cp(("arbitrary", "arbitrary")),
        name="inproj",
    )(xr, g, sh, sc, w)


PREP_TM = 256
QS_MLA = (MLA_NOPE + MLA_ROPE) ** -0.5 * LOG2E
QS_GQA = GQA_DH ** -0.5 * LOG2E


def _prep_kernel(cq_ref, ckv_ref, krz_ref, aq_ref, ak_ref, gq_ref, gkv_ref, wuq_ref, wk_ref, wv_ref,
                 gaq_ref, gak_ref, cg_ref, sg_ref, cm_ref, sm_ref,
                 qm_ref, km_ref, vm_ref, qg_ref, kg_ref):
    lane = lax.broadcasted_iota(I32, (PREP_TM, 128), 1)

    def rope(t, c, s, half):
        nxt = pltpu.roll(t, 128 - half, 1)
        prv = pltpu.roll(t, half, 1)
        return t * c + jnp.where((lane % (2 * half)) < half, nxt, prv) * s

    cm, sm = cm_ref[...], sm_ref[...]
    cg, sg = cg_ref[...], sg_ref[...]

    q = _dot(_rms(cq_ref[...].astype(F32), gq_ref[...]).astype(BF), wuq_ref[...])
    for h in range(MLA_H):
        a = 256 * h
        qm_ref[:, a:a + 128] = (q[:, a:a + 128] * QS_MLA).astype(BF)
        qm_ref[:, a + 128:a + 256] = (rope(q[:, a + 128:a + 256], cm, sm, MLA_ROPE // 4) * QS_MLA).astype(BF)

    n = _rms(ckv_ref[...].astype(F32), gkv_ref[...]).astype(BF)
    kn = _dot(n, wk_ref[...])
    vm_ref[...] = _dot(n, wv_ref[...]).astype(BF)
    kr = rope(krz_ref[...].astype(F32), cm, sm, MLA_ROPE // 4).astype(BF)
    for h in range(MLA_H):
        a = 256 * h
        km_ref[:, a:a + 128] = kn[:, 128 * h:128 * h + 128].astype(BF)
        km_ref[:, a + 128:a + 256] = kr

    for h in range(GQA_H):
        t = _rms(aq_ref[:, 128 * h:128 * h + 128].astype(F32), gaq_ref[...])
        qg_ref[:, 128 * h:128 * h + 128] = (rope(t, cg, sg, GQA_DH // 4) * QS_GQA).astype(BF)
    for h in range(GQA_KV):
        t = _rms(ak_ref[:, 128 * h:128 * h + 128].astype(F32), gak_ref[...])
        kg_ref[:, 128 * h:128 * h + 128] = rope(t, cg, sg, GQA_DH // 4).astype(BF)


def _rope_tables(S):
    pos = np.arange(S)
    rows, cols = pos // GRID_W, pos % GRID_W

    def tab(d, width):
        half = d // 2
        inv = ROPE_THETA ** (-np.arange(0, half, 2, dtype=np.float64) / half)
        ar, ac = rows[:, None] * inv[None], cols[:, None] * inv[None]
        c = np.concatenate([np.cos(ar), np.cos(ar), np.cos(ac), np.cos(ac)], 1)
        s = np.concatenate([-np.sin(ar), np.sin(ar), -np.sin(ac), np.sin(ac)], 1)
        ci = np.ones((CT, d)); si = np.zeros((CT, d))
        c = np.concatenate([c, ci], 0); s = np.concatenate([s, si], 0)
        pad = np.zeros((S + CT, width - d))
        return (jnp.asarray(np.concatenate([c, pad], 1), F32), jnp.asarray(np.concatenate([s, pad], 1), F32))

    return tab(GQA_DH, 128) + tab(MLA_ROPE, 128)


def _prep(proj, gq, gkv, wuq, wk, wv, gaq, gak, tabs, *, S, RL):
    R = proj.shape[0]
    tm = PREP_TM
    nlat = RL // tm
    spb = S // tm
    tab_idx = lambda i: (jnp.where(i < nlat, i % spb, spb), 0)
    col = lambda w, off: pl.BlockSpec((tm, w), lambda i: (i, off // w))
    full = lambda a: pl.BlockSpec(a.shape, lambda i: (0,) * a.ndim)
    tspec = pl.BlockSpec((tm, 128), tab_idx)
    widths = [MLA_H * 256, MLA_H * 256, MLA_H * MLA_DV, GQA_H * GQA_DH, GQA_KV * GQA_DH]
    return pl.pallas_call(
        _prep_kernel,
        grid=(R // tm,),
        in_specs=[col(MLA_QR, OFF_CQ), col(MLA_KVR, OFF_CKV), col(128, OFF_KRZ), col(768, OFF_AQ), col(256, OFF_AK),
                  full(gq), full(gkv), full(wuq), full(wk), full(wv), full(gaq), full(gak),
                  tspec, tspec, tspec, tspec],
        out_specs=[pl.BlockSpec((tm, w), lambda i: (i, 0)) for w in widths],
        out_shape=[jax.ShapeDtypeStruct((R, w), BF) for w in widths],
        compiler_params=_cp(("arbitrary",)),
        name="qk_prep",
    )(proj, proj, proj, proj, proj, gq, gkv, wuq, wk, wv, gaq, gak, *tabs)


GLA_TM = 256


def _gla_kernel(qf_ref, kf_ref, vf_ref, zf_ref, qb_ref, kb_ref, vb_ref, zb_ref, wg_ref, bg_ref, tri_ref,
                of_ref, ob_ref, st_scr):
    @pl.when(pl.program_id(1) == 0)
    def _():
        st_scr[...] = jnp.zeros_like(st_scr)

    dirs = ((qf_ref, kf_ref, vf_ref, zf_ref, of_ref), (qb_ref, kb_ref, vb_ref, zb_ref, ob_ref))
    las, sts = [], []
    for d, (_, _, _, z_ref, _) in enumerate(dirs):
        z = _dot(z_ref[...], wg_ref[d]) + bg_ref[d]
        las.append((jnp.minimum(z, 0.0) - jnp.log(1.0 + jnp.exp(-jnp.abs(z)))) * (1.0 / GLA_TAU))
        sts.append([st_scr[d, h] for h in range(GLA_H)])
    C = GLA_CHUNK
    nc = GLA_TM // C
    for cc in range(nc):
        for d, (q_ref, k_ref, v_ref, _, o_ref) in enumerate(dirs):
            tri = tri_ref[d]
            keep = tri > 0
            r0 = (cc if d == 0 else nc - 1 - cc) * C
            la = las[d][r0:r0 + C, :]
            h1 = la.astype(BF)
            r1 = la - h1.astype(F32)
            h2 = r1.astype(BF)
            h3 = (r1 - h2.astype(F32)).astype(BF)
            b = _dot(tri, h1) + _dot(tri, h2) + _dot(tri, h3)
            tot = jnp.sum(la, axis=0, keepdims=True)
            q = q_ref[r0:r0 + C, :].astype(F32)
            k = k_ref[r0:r0 + C, :].astype(F32)
            qd = (q * (GLA_DK ** -0.5) * jnp.exp(b)).astype(BF)
            ki = (k * jnp.exp(-b)).astype(BF)
            ke = (k * jnp.exp(tot - b)).astype(BF)
            dec = jnp.exp(tot)
            v = v_ref[r0:r0 + C, :]
            for h in range(GLA_H):
                sl = slice(GLA_DK * h, GLA_DK * (h + 1))
                sv = slice(GLA_DV * h, GLA_DV * (h + 1))
                att = jnp.where(keep, _dot_nt(qd[:, sl], ki[:, sl]), 0.0).astype(BF)
                st = sts[d][h]
                o_ref[r0:r0 + C, sv] = _dot(att, v[:, sv]) + _dot_nt(qd[:, sl], st.astype(BF))
                sts[d][h] = st * dec[:, sl] + _dot_tn(v[:, sv], ke[:, sl])
    for d in range(2):
        for h in range(GLA_H):
            st_scr[d, h] = sts[d][h]


def _gla(proj, wg, bg, tri, *, B, S, RL):
    R = proj.shape[0]
    tm = GLA_TM
    nj = S // tm + 1
    lat0 = lambda b: b * (S // tm)

    def rb(d):
        def f(b, j):
            jl = (j - 1) if d == 0 else (S // tm - j)
            return jnp.where(j == 0, RL // tm + b, lat0(b) + jl)
        return f

    col = lambda d, w, off: pl.BlockSpec((tm, w), lambda b, j: (rb(d)(b, j), off // w))
    full = lambda a: pl.BlockSpec(a.shape, lambda b, j: (0,) * a.ndim)
    ins = lambda d: [col(d, 256, OFF_GQ), col(d, 256, OFF_GK), col(d, 512, OFF_GV), col(d, 128, OFF_KRZ)]
    out = lambda d: pl.BlockSpec((tm, GLA_H * GLA_DV), lambda b, j: (rb(d)(b, j), 0))
    return pl.pallas_call(
        _gla_kernel,
        grid=(B, nj),
        in_specs=ins(0) + ins(1) + [full(wg), full(bg), full(tri)],
        out_specs=[out(0), out(1)],
        out_shape=[jax.ShapeDtypeStruct((R, GLA_H * GLA_DV), F32)] * 2,
        scratch_shapes=[pltpu.VMEM((2, GLA_H, GLA_DV, GLA_DK), F32)],
        compiler_params=_cp(("arbitrary", "arbitrary")),
        name="gla",
    )(*([proj] * 8), wg, bg, tri)


def _flash_kernel(*refs, nseg, seg_lens, tq, dv, ck):
    q_ref = refs[0]
    k_refs = refs[1:1 + nseg]
    v_refs = refs[1 + nseg:1 + 2 * nseg]
    o_ref = refs[1 + 2 * nseg]
    s_scr = refs[2 + 2 * nseg]
    q = q_ref[...]
    m = jnp.full((tq, 128), -jnp.inf, F32)
    chunks = []
    off = 0
    for si in range(nseg):
        for c0 in range(0, seg_lens[si], ck):
            cl = min(ck, seg_lens[si] - c0)
            s = _dot_nt(q, k_refs[si][c0:c0 + cl, :])
            s_scr[:, off:off + cl] = s
            for l0 in range(0, cl, 128):
                m = jnp.maximum(m, s[:, l0:l0 + 128])
            chunks.append((si, c0, cl, off))
            off += cl
    mrow = jnp.max(m, axis=1, keepdims=True)
    l = jnp.zeros((tq, 128), F32)
    acc = jnp.zeros((tq, dv), F32)
    for si, c0, cl, off in chunks:
        p = jnp.exp2(s_scr[:, off:off + cl] - mrow)
        for l0 in range(0, cl, 128):
            l = l + p[:, l0:l0 + 128]
        acc = acc + _dot(p.astype(BF), v_refs[si][c0:c0 + cl, :])
    o_ref[...] = (acc / jnp.sum(l, axis=1, keepdims=True)).astype(o_ref.dtype)


def _attention(qa, ka, va, *, B, S, RL, hq, group, dk, dv, vbase, ctx_queries, name):
    cb = RL // CT
    kh = lambda h: h // group
    if ctx_queries:
        tq, nq, n_out = CT, 1, B * CT
        q_spec = pl.BlockSpec((tq, dk), lambda b, h, i: (cb + b, h))
        o_spec = pl.BlockSpec((tq, dv), lambda b, h, i: (b, h))
        k_specs = [pl.BlockSpec((CT, dk), lambda b, h, i: (cb + b, kh(h)))]
        v_specs = [pl.BlockSpec((CT, dv), lambda b, h, i: (cb + b, vbase + kh(h)))]
        seg_lens = (CT,)
    else:
        tq, nq, n_out = 256, S // 256, RL
        q_spec = pl.BlockSpec((tq, dk), lambda b, h, i: (b * nq + i, h))
        o_spec = pl.BlockSpec((tq, dv), lambda b, h, i: (b * nq + i, h))
        k_specs = [pl.BlockSpec((CT, dk), lambda b, h, i: (cb + b, kh(h))),
                   pl.BlockSpec((S, dk), lambda b, h, i: (b, kh(h)))]
        v_specs = [pl.BlockSpec((CT, dv), lambda b, h, i: (cb + b, vbase + kh(h))),
                   pl.BlockSpec((S, dv), lambda b, h, i: (b, vbase + kh(h)))]
        seg_lens = (CT, S)
    nseg = len(seg_lens)
    kern = functools.partial(_flash_kernel, nseg=nseg, seg_lens=seg_lens, tq=tq, dv=dv, ck=512)
    return pl.pallas_call(
        kern,
        grid=(B, hq, nq),
        in_specs=[q_spec] + k_specs + v_specs,
        out_specs=o_spec,
        out_shape=jax.ShapeDtypeStruct((n_out, hq * dv), BF),
        scratch_shapes=[pltpu.VMEM((tq, sum(seg_lens)), F32)],
        compiler_params=_cp(("arbitrary", "arbitrary", "arbitrary")),
        name=name,
    )(qa, *([ka] * nseg), *([va] * nseg))


WOUT_TM = 256
COUNT_BITS = 20


def _route(logit, rb):
    T = logit.shape[1]
    gsz = N_EXP // N_GRP
    scores = _sigmoid(logit)
    sel = scores + rb
    ninf = jnp.float32(-jnp.inf)
    sel3 = sel.reshape(N_GRP, gsz, T)
    idx3 = lax.broadcasted_iota(I32, (N_GRP, gsz, T), 1)
    m1 = jnp.max(sel3, axis=1, keepdims=True)
    first = jnp.min(jnp.where(sel3 == m1, idx3, gsz), axis=1, keepdims=True)
    m2 = jnp.max(jnp.where(idx3 == first, ninf, sel3), axis=1, keepdims=True)
    gs = (m1 + m2).reshape(N_GRP, T)
    gi = lax.broadcasted_iota(I32, (N_GRP, T), 0)
    grank = jnp.zeros((N_GRP, T), I32)
    for j in range(N_GRP):
        row = gs[j:j + 1, :]
        grank = grank + ((row > gs) | ((row == gs) & (j < gi))).astype(I32)
    gkeep = (grank < TOPK_GRP).reshape(N_GRP, 1, T)
    selm = jnp.where(jnp.broadcast_to(gkeep, (N_GRP, gsz, T)), sel3, ninf).reshape(N_EXP, T)
    ei = lax.broadcasted_iota(I32, (N_EXP, T), 0)
    erank = jnp.zeros((N_EXP, T), I32)
    for j in range(N_EXP):
        row = selm[j:j + 1, :]
        erank = erank + ((row > selm) | ((row == selm) & (j < ei))).astype(I32)
    chosen = erank < TOP_K
    w = jnp.where(chosen, scores, 0.0)
    wd = w / jnp.sum(w, axis=0, keepdims=True) * ROUTED_SCALE
    return wd, chosen


def _count_to_int(x, nbits):
    out = jnp.zeros(x.shape, I32)
    for b in reversed(range(nbits)):
        ge = x >= float(1 << b)
        out = out + jnp.where(ge, 1 << b, 0)
        x = x - jnp.where(ge, float(1 << b), 0.0)
    return out


def _store_tile_rows(ref, packed, tm):
    for cblk in range(HALF // 128):
        ref[pl.ds(cblk, tm, stride=8), :] = packed[:, 128 * cblk:128 * (cblk + 1)]


def _load_tile_rows(ref, tm):
    los, his = [], []
    for cblk in range(HALF // 128):
        lo, hi = _unpack_halves(ref[pl.ds(cblk, tm, stride=8), :])
        los.append(lo.astype(BF))
        his.append(hi.astype(BF))
    return jnp.concatenate(los, axis=1), jnp.concatenate(his, axis=1)


def _wout_kernel(of_ref, ob_ref, gr_ref, mla_ref, gqa_ref, x_ref, w_ref, gout_ref, gpm_ref, gtm_ref,
                 gpf_ref, shf_ref, scf_ref, rwt_ref, rb_ref, ust_ref, lst_ref,
                 xn_ref, hfp_ref, r8_ref, e8_ref, w8_ref, cnt_ref, cnt_scr):
    @pl.when(pl.program_id(0) == 0)
    def _():
        cnt_scr[...] = jnp.zeros_like(cnt_scr)

    og = of_ref[...] + ob_ref[...]
    gr = gr_ref[...].astype(F32)
    parts = []
    for h in range(GLA_H):
        sv = slice(GLA_DV * h, GLA_DV * (h + 1))
        g = gr[:, sv]
        parts.append((_rms(og[:, sv], gout_ref[...]) * (g * _sigmoid(g))).astype(BF))
    gla = jnp.concatenate(parts, axis=1)
    n_g, n_m = GLA_H * GLA_DV, MLA_H * MLA_DV
    y = (_dot(gla, w_ref[0:n_g, :]) + _dot(mla_ref[...], w_ref[n_g:n_g + n_m, :])
         + _dot(gqa_ref[...], w_ref[n_g + n_m:, :]))
    xn = x_ref[...] + gtm_ref[0] * _rms(y, gpm_ref[...])
    xn_ref[...] = xn
    hf = _rms(xn, gpf_ref[...]) * (1.0 + scf_ref[0]) + shf_ref[0]
    _store_tile_rows(hfp_ref, _pack_halves(hf), WOUT_TM)
    hh, hl = _split2(hf)
    rh, rl = _split2(rwt_ref[...])
    logit = _dot_nt(rh, hh) + _dot_nt(rl, hh) + _dot_nt(rh, hl)
    wd, chosen = _route(logit, rb_ref[...])

    cb = jnp.where(chosen, 1.0, 0.0)
    before = _dot(cb.astype(BF), ust_ref[...])
    rank = cnt_scr[:, 0:1] + before
    slot = _dot(lst_ref[...], cb.astype(BF))
    eid = lax.broadcasted_iota(I32, chosen.shape, 0).astype(F32)
    r8, e8, w8 = [], [], []
    for k in range(TOP_K):
        pick = chosen & (slot == float(k))
        r8.append(jnp.sum(jnp.where(pick, rank, 0.0), axis=0, keepdims=True))
        e8.append(jnp.sum(jnp.where(pick, eid, 0.0), axis=0, keepdims=True))
        w8.append(jnp.sum(jnp.where(pick, wd, 0.0), axis=0, keepdims=True))
    r8_ref[...] = _count_to_int(jnp.concatenate(r8, axis=0), COUNT_BITS)
    e8_ref[...] = _count_to_int(jnp.concatenate(e8, axis=0), 6)
    w8_ref[...] = jnp.concatenate(w8, axis=0)
    cnt_scr[...] = cnt_scr[...] + jnp.sum(cb, axis=1, keepdims=True)
    cnt_ref[...] = _count_to_int(cnt_scr[...], COUNT_BITS)


def _wout(gla_o, proj, mla_o, gqa_o, xr, w, gout, gpm, gtm, gpf, shf, scf, rwt, rb, *, n_rows, tiles_per_batch, nb):
    tm = WOUT_TM
    mod_idx = lambda i: (jnp.minimum(i // tiles_per_batch, nb), 0, 0)
    row = lambda w_: pl.BlockSpec((tm, w_), lambda i: (i, 0))
    full = lambda a: pl.BlockSpec(a.shape, lambda i: (0,) * a.ndim)
    mspec = pl.BlockSpec((1, 1, D), mod_idx)
    it = np.arange(tm)
    ust = jnp.asarray(it[:, None] < it[None, :], BF)
    ie = np.arange(N_EXP)
    lst = jnp.asarray(ie[None, :] < ie[:, None], BF)
    k8 = pl.BlockSpec((TOP_K, tm), lambda i: (0, i))
    return pl.pallas_call(
        _wout_kernel,
        grid=(n_rows // tm,),
        in_specs=[row(512), row(512),
                  pl.BlockSpec((tm, 512), lambda i: (i, OFF_GR // 512)),
                  row(768), row(768), row(D), full(w), full(gout), full(gpm), mspec,
                  full(gpf), mspec, mspec, full(rwt), full(rb), full(ust), full(lst)],
        out_specs=[row(D), pl.BlockSpec((tm * 8, 128), lambda i: (i, 0)), k8, k8, k8,
                   pl.BlockSpec((N_EXP, 128), lambda i: (0, 0))],
        out_shape=[jax.ShapeDtypeStruct((n_rows, D), F32), jax.ShapeDtypeStruct((n_rows * 8, 128), PACKED),
                   jax.ShapeDtypeStruct((TOP_K, n_rows), I32), jax.ShapeDtypeStruct((TOP_K, n_rows), I32),
                   jax.ShapeDtypeStruct((TOP_K, n_rows), F32), jax.ShapeDtypeStruct((N_EXP, 128), I32)],
        scratch_shapes=[pltpu.VMEM((N_EXP, 128), F32)],
        compiler_params=_cp(("arbitrary",)),
        name="wout_route",
    )(gla_o[0], gla_o[1], proj, mla_o, gqa_o, xr, w, gout, gpm, gtm, gpf, shf, scf, rwt, rb, ust, lst)


MOE_TM = 512
DISP_TM = 256
COMB_TM = 128


def _tile_row(ref, r):
    return ref.at[pl.ds(pl.multiple_of(r * 8, 8), 8), :]


def _dispatch_kernel(cnt_ref, pst_ref, ntl_ref, nu_ref, pos_ref, hfp_ref, xs_ref, zero_scr, sem, zsem):
    i = pl.program_id(0)
    tm = DISP_TM
    tile_rows = MOE_TM * 8
    n_tiles = xs_ref.shape[0] // tile_rows

    def issue(t, carry):
        for k in range(TOP_K):
            p = pos_ref[0, 0, k * tm + t]
            pltpu.make_async_copy(_tile_row(hfp_ref, t), _tile_row(xs_ref, p), sem).start(priority=k % 2)
        return carry

    lax.fori_loop(0, tm, issue, 0, unroll=2)

    @pl.when(i == 0)
    def _():
        zero_scr[...] = _packed_zeros(zero_scr.shape)

        def per_expert(e, carry):
            cnt = cnt_ref[e]
            padlen = ntl_ref[e] * MOE_TM - cnt
            off = pst_ref[e] + cnt

            pieces = []
            bit = MOE_TM // 2
            while bit >= 1:
                take = padlen & bit
                dst = xs_ref.at[pl.ds(pl.multiple_of(off * 8, 8), bit * 8), :]
                pieces.append((take != 0, pltpu.make_async_copy(zero_scr.at[pl.ds(0, bit * 8), :], dst, zsem)))
                off = off + take
                bit //= 2
            for go, cp in pieces:
                pl.when(go)(cp.start)
            for go, cp in pieces:
                pl.when(go)(cp.wait)
            return carry

        lax.fori_loop(0, N_EXP, per_expert, 0)

        def ztile(t):
            dst = xs_ref.at[pl.ds(pl.multiple_of(t * tile_rows, tile_rows), tile_rows), :]
            return pltpu.make_async_copy(zero_scr, dst, zsem)

        lax.fori_loop(nu_ref[0], n_tiles, lambda t, cr: (ztile(t).start(), cr)[1], 0)
        lax.fori_loop(nu_ref[0], n_tiles, lambda t, cr: (ztile(t).wait(), cr)[1], 0)

    for k in range(TOP_K):
        pltpu.make_async_copy(hfp_ref, xs_ref.at[pl.ds(0, tm * 8), :], sem).wait()


def _dispatch(cnt, pst, ntl, n_used, pos_t, hfp, *, n_rows_sorted):
    n = hfp.shape[0] // 8
    tm = DISP_TM
    grid_spec = pltpu.PrefetchScalarGridSpec(
        num_scalar_prefetch=4,
        grid=(n // tm,),
        in_specs=[pl.BlockSpec((1, 1, tm * TOP_K), lambda i, *_: (i, 0, 0), memory_space=pltpu.SMEM),
                  pl.BlockSpec((tm * 8, 128), lambda i, *_: (i, 0))],
        out_specs=pl.BlockSpec(memory_space=pl.ANY),
        scratch_shapes=[pltpu.VMEM((MOE_TM * 8, 128), PACKED), pltpu.SemaphoreType.DMA, pltpu.SemaphoreType.DMA],
    )
    return pl.pallas_call(
        _dispatch_kernel,
        grid_spec=grid_spec,
        out_shape=jax.ShapeDtypeStruct((n_rows_sorted * 8, 128), PACKED),
        compiler_params=_cp(("arbitrary",)),
        name="moe_dispatch",
    )(cnt, pst, ntl, n_used, pos_t, hfp)


def _gmm_kernel(te_ref, nu_ref, x_ref, w1_ref, w3_ref, w2_ref, y_ref, w1b, w3b, w2b):
    i = pl.program_id(0)
    used = i < nu_ref[0]

    @pl.when(used)
    def _():
        changed = jnp.logical_or(i == 0, te_ref[jnp.maximum(i - 1, 0)] != te_ref[i])

        @pl.when(changed)
        def _():
            w1b[...] = w1_ref[0].astype(BF)
            w3b[...] = w3_ref[0].astype(BF)
            w2b[...] = w2_ref[0].astype(BF)

        xa, xb = _load_tile_rows(x_ref, MOE_TM)
        h1 = _dot(xa, w1b[0:HALF, :]) + _dot(xb, w1b[HALF:, :])
        h3 = _dot(xa, w3b[0:HALF, :]) + _dot(xb, w3b[HALF:, :])
        h = (h1 * _sigmoid(h1) * h3).astype(BF)
        _store_tile_rows(y_ref, _pack_halves(_dot(h, w2b[...])), MOE_TM)

    @pl.when(jnp.logical_not(used))
    def _():
        y_ref[...] = _packed_zeros(y_ref.shape)


def _gmm(tile_e, n_used, xs, w1, w3, w2, layer):
    tm = MOE_TM
    n_tiles = xs.shape[0] // (tm * 8)
    last = lambda i, nu: jnp.minimum(i, nu[0] - 1)
    wsel = lambda i, te, nu: (layer * N_EXP + te[last(i, nu)], 0, 0)
    grid_spec = pltpu.PrefetchScalarGridSpec(
        num_scalar_prefetch=2,
        grid=(n_tiles,),
        in_specs=[pl.BlockSpec((tm * 8, 128), lambda i, te, nu: (last(i, nu), 0)),
                  pl.BlockSpec((1, D, D_EXP), wsel),
                  pl.BlockSpec((1, D, D_EXP), wsel),
                  pl.BlockSpec((1, D_EXP, D), wsel)],
        out_specs=pl.BlockSpec((tm * 8, 128), lambda i, te, nu: (i, 0)),
        scratch_shapes=[pltpu.VMEM((D, D_EXP), BF), pltpu.VMEM((D, D_EXP), BF), pltpu.VMEM((D_EXP, D), BF)],
    )
    return pl.pallas_call(
        _gmm_kernel,
        grid_spec=grid_spec,
        out_shape=jax.ShapeDtypeStruct(xs.shape, PACKED),
        compiler_params=_cp(("arbitrary",)),
        name="moe_experts",
    )(tile_e, n_used, xs, w1, w3, w2)


def _combine_kernel(pos_ref, posn_ref, w8_ref, hfp_ref, x_ref, s1_ref, s3_ref, s2_ref, gpost_ref, gtf_ref, ys_ref,
                    o_ref, ybuf_a, ybuf_b, sem_a, sem_b):
    tm = COMB_TM
    i = pl.program_id(0)
    last = pl.num_programs(0) - 1

    def gather(pref, buf, sem):
        for t in range(tm):
            for k in range(TOP_K):
                p = pref[0, 0, k * tm + t]
                pltpu.make_async_copy(_tile_row(ys_ref, p), buf.at[k, pl.ds(t * 8, 8), :], sem).start(priority=k % 2)

    def drain(buf, sem):
        for k in range(TOP_K):
            pltpu.make_async_copy(ys_ref.at[pl.ds(0, tm * 8), :], buf.at[k], sem).wait()

    @pl.when(i == 0)
    def _():
        gather(pos_ref, ybuf_a, sem_a)

    def body(cur, sem_c, nxt, sem_n):
        gather(posn_ref, nxt, sem_n)
        xa, xb = _load_tile_rows(hfp_ref, tm)
        h1 = _dot(xa, s1_ref[0:HALF, :]) + _dot(xb, s1_ref[HALF:, :])
        h3 = _dot(xa, s3_ref[0:HALF, :]) + _dot(xb, s3_ref[HALF:, :])
        sh = _dot((h1 * _sigmoid(h1) * h3).astype(BF), s2_ref[...])
        drain(cur, sem_c)
        w8 = w8_ref[...]
        wk = [w8[:, k:k + 1] for k in range(TOP_K)]
        los, his = [], []
        for cblk in range(HALF // 128):
            a_lo = sh[:, 128 * cblk:128 * (cblk + 1)]
            a_hi = sh[:, HALF + 128 * cblk:HALF + 128 * (cblk + 1)]
            for k in range(TOP_K):
                lo, hi = _unpack_halves(cur[k, pl.ds(cblk, tm, stride=8), :])
                a_lo = a_lo + wk[k] * lo
                a_hi = a_hi + wk[k] * hi
            los.append(a_lo)
            his.append(a_hi)
        yf = jnp.concatenate(los + his, axis=1)
        o_ref[...] = x_ref[...] + gtf_ref[0] * _rms(yf, gpost_ref[...])

        @pl.when(i == last)
        def _():
            drain(nxt, sem_n)

    @pl.when(i % 2 == 0)
    def _():
        body(ybuf_a, sem_a, ybuf_b, sem_b)

    @pl.when(i % 2 == 1)
    def _():
        body(ybuf_b, sem_b, ybuf_a, sem_a)


def _combine(pos_t, w8, hfp, xn, s1, s3, s2, gpost, gtf, ys, *, tiles_per_batch, nb):
    n = hfp.shape[0] // 8
    tm = COMB_TM
    nt = n // tm
    mod_idx = lambda i: (jnp.minimum(i // tiles_per_batch, nb), 0, 0)
    full = lambda a: pl.BlockSpec(a.shape, lambda i: (0,) * a.ndim)
    buf = pltpu.VMEM((TOP_K, tm * 8, 128), PACKED)
    return pl.pallas_call(
        _combine_kernel,
        grid=(nt,),
        in_specs=[pl.BlockSpec((1, 1, tm * TOP_K), lambda i: (i, 0, 0), memory_space=pltpu.SMEM),
                  pl.BlockSpec((1, 1, tm * TOP_K), lambda i: (jnp.minimum(i + 1, nt - 1), 0, 0),
                               memory_space=pltpu.SMEM),
                  pl.BlockSpec((tm, TOP_K), lambda i: (i, 0)),
                  pl.BlockSpec((tm * 8, 128), lambda i: (i, 0)),
                  pl.BlockSpec((tm, D), lambda i: (i, 0)),
                  full(s1), full(s3), full(s2), full(gpost),
                  pl.BlockSpec((1, 1, D), mod_idx),
                  pl.BlockSpec(memory_space=pl.ANY)],
        out_specs=pl.BlockSpec((tm, D), lambda i: (i, 0)),
        out_shape=jax.ShapeDtypeStruct((n, D), F32),
        scratch_shapes=[buf, buf, pltpu.SemaphoreType.DMA, pltpu.SemaphoreType.DMA],
        compiler_params=_cp(("arbitrary",)),
        name="moe_combine",
    )(pos_t, pos_t, w8, hfp, xn, s1, s3, s2, gpost, gtf, ys)


def _dispatch_tables(counts, r8, e8):
    n = r8.shape[1]
    ntl = (counts + MOE_TM - 1) // MOE_TM
    tend = jnp.cumsum(ntl)
    pst = (tend - ntl) * MOE_TM
    onehot = e8[None] == jnp.arange(N_EXP, dtype=I32)[:, None, None]
    pos = r8 + jnp.sum(jnp.where(onehot, pst[:, None, None], 0), axis=0)
    n_tiles = -(-(n * TOP_K) // MOE_TM) + N_EXP
    tile_e = jnp.sum((jnp.arange(n_tiles, dtype=I32)[:, None] >= tend[None, :]).astype(I32), axis=1)
    tile_e = jnp.minimum(tile_e, N_EXP - 1)
    return pst.astype(I32), ntl.astype(I32), pos.astype(I32), tile_e.astype(I32), tend[-1:].astype(I32), n_tiles


def _pos_blocks(pos, tm):
    k, n = pos.shape
    return pos.reshape(k, n // tm, tm).transpose(1, 0, 2).reshape(n // tm, 1, k * tm)


def _inproj_weight(w_in):
    offs = np.cumsum((0, 256, 256, 512, 512, 16, 16, 384, 512, 64, 768, 256, 256))
    p = [w_in[:, offs[i]:offs[i + 1]] for i in range(12)]
    gq, gk, gv, gr, gzf, gzb, cq, ckv, kr, aq, ak, av = p
    krz = jnp.concatenate([kr, gzf, gzb, jnp.zeros((D, 32), w_in.dtype)], axis=1)
    return jnp.concatenate([gv, gr, ckv, aq, cq, krz, gq, gk, ak, av], axis=1).astype(BF)


def _mla_weights(w_uq, w_ukv):
    wq = w_uq.reshape(MLA_QR, MLA_H, MLA_NOPE + MLA_ROPE)
    wq = jnp.concatenate([wq, jnp.zeros((MLA_QR, MLA_H, 256 - MLA_NOPE - MLA_ROPE), wq.dtype)], axis=2)
    wkv = w_ukv.reshape(MLA_KVR, MLA_H, MLA_NOPE + MLA_DV)
    wk = wkv[:, :, :MLA_NOPE].reshape(MLA_KVR, MLA_H * MLA_NOPE)
    wv = wkv[:, :, MLA_NOPE:].reshape(MLA_KVR, MLA_H * MLA_DV)
    return wq.reshape(MLA_QR, MLA_H * 256).astype(BF), wk.astype(BF), wv.astype(BF)


def _gla_gate_weights(wf, bf, wb, bb):
    z = jnp.zeros((128, GLA_H * GLA_DK), F32)
    wg = jnp.stack([z.at[64:80].set(wf), z.at[80:96].set(wb)]).astype(BF)
    bg = jnp.stack([bf, bb]).reshape(2, 1, GLA_H * GLA_DK)
    return wg, bg


def kernel(x, c, ctx, c_ctx, w_mod, b_mod, g_pre_mix, g_post_mix, g_pre_ffn, g_post_ffn, w_in, gla_wg2_f, gla_bg_f, gla_wg2_b, gla_bg_b, gla_g_out, mla_g_q, mla_w_uq, mla_g_kv, mla_w_ukv, gqa_g_q, gqa_g_k, w_out, router_w, router_b, exp_w1, exp_w3, exp_w2, sh_w1, sh_w3, sh_w2):
    B, S, _ = x.shape
    depth = w_mod.shape[0]
    assert ctx.shape[1] == CT and S % 256 == 0 and B < 8 and B * (S + CT) < (1 << COUNT_BITS)
    RL, RC = B * S, B * CT
    R = RL + RC

    xr = jnp.concatenate([x.reshape(RL, D), ctx.reshape(RC, D)], axis=0)
    cvecs = jnp.concatenate([c, c_ctx[None], jnp.zeros((8 - B - 1, D), F32)], axis=0)
    mods = _modulation(cvecs, w_mod, b_mod)
    tabs = _rope_tables(S)
    ii = np.arange(GLA_CHUNK)
    tri = jnp.asarray(np.stack([ii[:, None] >= ii[None, :], ii[:, None] <= ii[None, :]]), BF)

    in_tm = 1024 if (S % 1024 == 0 and RC % 1024 == 0) else 256
    row2 = lambda v: v.reshape(1, -1)
    ew1 = exp_w1.reshape(depth * N_EXP, D, D_EXP)
    ew3 = exp_w3.reshape(depth * N_EXP, D, D_EXP)
    ew2 = exp_w2.reshape(depth * N_EXP, D_EXP, D)

    for l in range(depth):
        want_ctx = l < depth - 1
        m = mods[l].reshape(8, 6, 1, D)
        sh_m, sc_m, gt_m, sh_f, sc_f, gt_f = [m[:, i] for i in range(6)]

        proj = _inproj(xr, row2(g_pre_mix[l]), sh_m, sc_m, _inproj_weight(w_in[l]),
                       tm=in_tm, tiles_per_batch=S // in_tm, nb=B)
        wuq, wk, wv = _mla_weights(mla_w_uq[l], mla_w_ukv[l])
        qm, km, vm, qg, kg = _prep(proj, row2(mla_g_q[l]), row2(mla_g_kv[l]), wuq, wk, wv,
                                   row2(gqa_g_q[l]), row2(gqa_g_k[l]), tabs, S=S, RL=RL)
        wg, bg = _gla_gate_weights(gla_wg2_f[l], gla_bg_f[l], gla_wg2_b[l], gla_bg_b[l])
        gla_o = _gla(proj, wg, bg, tri, B=B, S=S, RL=RL)

        att = functools.partial(_attention, B=B, S=S, RL=RL)
        mla = functools.partial(att, qm, km, vm, hq=MLA_H, group=1, dk=256, dv=MLA_DV, vbase=0)
        gqa = functools.partial(att, qg, kg, proj, hq=GQA_H, group=GQA_H // GQA_KV, dk=GQA_DH, dv=GQA_DH,
                                vbase=OFF_AV // GQA_DH)
        mla_o = mla(ctx_queries=False, name="mla_lat")
        gqa_o = gqa(ctx_queries=False, name="gqa_lat")
        if want_ctx:
            mla_o = jnp.concatenate([mla_o, mla(ctx_queries=True, name="mla_ctx")], axis=0)
            gqa_o = jnp.concatenate([gqa_o, gqa(ctx_queries=True, name="gqa_ctx")], axis=0)
        n_rows = R if want_ctx else RL

        xn, hfp, r8, e8, w8, cnt = _wout(gla_o, proj, mla_o, gqa_o, xr, w_out[l].astype(BF), row2(gla_g_out[l]),
                                         row2(g_post_mix[l]), gt_m, row2(g_pre_ffn[l]), sh_f, sc_f,
                                         router_w[l].T, router_b[l].reshape(N_EXP, 1),
                                         n_rows=n_rows, tiles_per_batch=S // WOUT_TM, nb=B)

        counts = cnt[:, 0]
        pst, ntl, pos, tile_e, n_used, n_tiles = _dispatch_tables(counts, r8, e8)
        xs = _dispatch(counts, pst, ntl, n_used, _pos_blocks(pos, DISP_TM), hfp, n_rows_sorted=n_tiles * MOE_TM)
        ys = _gmm(tile_e, n_used, xs, ew1, ew3, ew2, l)
        xo = _combine(_pos_blocks(pos, COMB_TM), w8.T, hfp, xn,
                      sh_w1[l].astype(BF), sh_w3[l].astype(BF), sh_w2[l].astype(BF),
                      row2(g_post_ffn[l]), gt_f, ys, tiles_per_batch=S // COMB_TM, nb=B)
        xr = xo
    return xr[:RL].reshape(B, S, D)
```

```python
import functools
import math

import numpy as np
import jax
import jax.numpy as jnp
from jax import lax
from jax.experimental import pallas as pl
from jax.experimental.pallas import tpu as pltpu

F32 = jnp.float32
BF = jnp.bfloat16
PACKED = jnp.uint32
I32 = jnp.int32

D = 2048
GRID_W = 64
CT = 256
ROPE_THETA = 10000.0
EPS = 1e-6
GLA_H, GLA_DK, GLA_DV, GLA_RANK, GLA_TAU, GLA_CHUNK = 4, 64, 128, 16, 16.0, 64
MLA_H, MLA_QR, MLA_KVR, MLA_NOPE, MLA_ROPE, MLA_DV = 6, 384, 512, 128, 64, 128
GQA_H, GQA_KV, GQA_DH = 6, 2, 128
N_EXP, TOP_K, N_GRP, TOPK_GRP, D_EXP, D_SH = 64, 8, 8, 4, 512, 512
ROUTED_SCALE = 2.5
LOG2E = math.log2(math.e)
HALF = D // 2

OFF_GV, OFF_GR, OFF_CKV, OFF_AQ, OFF_CQ, OFF_KRZ, OFF_GQ, OFF_GK, OFF_AK, OFF_AV = (
    0, 512, 1024, 1536, 2304, 2688, 2816, 3072, 3328, 3584)
PROJ_W = 3840
VMEM_LIMIT = 56 * 1024 * 1024


def _cp(sem, vmem=None):
    return pltpu.CompilerParams(dimension_semantics=sem, vmem_limit_bytes=vmem or VMEM_LIMIT)


def _dot(a, b):
    return jnp.dot(a, b, preferred_element_type=F32)


def _dot_nt(a, b):
    return lax.dot_general(a, b, (((1,), (1,)), ((), ())), preferred_element_type=F32)


def _dot_tn(a, b):
    return lax.dot_general(a, b, (((0,), (0,)), ((), ())), preferred_element_type=F32)


def _sigmoid(x):
    return 1.0 / (1.0 + jnp.exp(-x))


def _rms(x, g):
    return x * lax.rsqrt(jnp.mean(x * x, axis=-1, keepdims=True) + EPS) * g


def _split2(x):
    hi = x.astype(BF)
    return hi, (x - hi.astype(F32)).astype(BF)


def _pack_halves(y):
    return pltpu.pack_elementwise([y[:, :HALF], y[:, HALF:]], packed_dtype=BF)


def _packed_zeros(shape):
    z = jnp.zeros(shape, F32)
    return pltpu.pack_elementwise([z, z], packed_dtype=BF)


def _unpack_halves(w):
    lo = pltpu.unpack_elementwise(w, index=0, packed_dtype=BF, unpacked_dtype=F32)
    hi = pltpu.unpack_elementwise(w, index=1, packed_dtype=BF, unpacked_dtype=F32)
    return lo, hi


def _mod_kernel(a_ref, w_ref, b_ref, o_ref):
    a = a_ref[...]
    a = a * _sigmoid(a)
    ah, al = _split2(a)
    wh, wl = _split2(w_ref[0])
    o_ref[0] = _dot(ah, wh) + _dot(al, wh) + _dot(ah, wl) + b_ref[0]


def _modulation(cvecs, w_mod, b_mod):
    L, _, n = w_mod.shape
    tn = 1024
    return pl.pallas_call(
        _mod_kernel,
        grid=(L, n // tn),
        in_specs=[pl.BlockSpec((8, D), lambda l, j: (0, 0)),
                  pl.BlockSpec((1, D, tn), lambda l, j: (l, 0, j)),
                  pl.BlockSpec((1, 1, tn), lambda l, j: (l, 0, j))],
        out_specs=pl.BlockSpec((1, 8, tn), lambda l, j: (l, 0, j)),
        out_shape=jax.ShapeDtypeStruct((L, 8, n), F32),
        compiler_params=_cp(("arbitrary", "arbitrary")),
        name="modulation",
    )(cvecs, w_mod, b_mod.reshape(L, 1, n))


def _inproj_kernel(x_ref, g_ref, sh_ref, sc_ref, w_ref, o_ref, a_scr):
    @pl.when(pl.program_id(1) == 0)
    def _():
        y = _rms(x_ref[...], g_ref[...])
        a_scr[...] = (y * (1.0 + sc_ref[0]) + sh_ref[0]).astype(BF)

    o_ref[...] = _dot(a_scr[...], w_ref[...]).astype(o_ref.dtype)


def _inproj(xr, g, sh, sc, w, *, tm, tiles_per_batch, nb):
    R = xr.shape[0]
    tn = 768
    mod_idx = lambda i, j: (jnp.minimum(i // tiles_per_batch, nb), 0, 0)
    return pl.pallas_call(
        _inproj_kernel,
        grid=(R // tm, PROJ_W // tn),
        in_specs=[pl.BlockSpec((tm, D), lambda i, j: (i, 0)),
                  pl.BlockSpec((1, D), lambda i, j: (0, 0)),
                  pl.BlockSpec((1, 1, D), mod_idx),
                  pl.BlockSpec((1, 1, D), mod_idx),
                  pl.BlockSpec((D, tn), lambda i, j: (0, j))],
        out_specs=pl.BlockSpec((tm, tn), lambda i, j: (i, j)),
        out_shape=jax.ShapeDtypeStruct((R, PROJ_W), BF),
        scratch_shapes=[pltpu.VMEM((tm, D), BF)],
        compiler_params=_cp(("arbitrary", "arbitrary")),
        name="inproj",
    )(xr, g, sh, sc, w)


PREP_TM = 256
QS_MLA = (MLA_NOPE + MLA_ROPE) ** -0.5 * LOG2E
QS_GQA = GQA_DH ** -0.5 * LOG2E


def _prep_kernel(cq_ref, ckv_ref, krz_ref, aq_ref, ak_ref, gq_ref, gkv_ref, wuq_ref, wk_ref, wv_ref,
                 gaq_ref, gak_ref, cg_ref, sg_ref, cm_ref, sm_ref,
                 qm_ref, km_ref, vm_ref, qg_ref, kg_ref):
    lane = lax.broadcasted_iota(I32, (PREP_TM, 128), 1)

    def rope(t, c, s, half):
        nxt = pltpu.roll(t, 128 - half, 1)
        prv = pltpu.roll(t, half, 1)
        return t * c + jnp.where((lane % (2 * half)) < half, nxt, prv) * s

    cm, sm = cm_ref[...], sm_ref[...]
    cg, sg = cg_ref[...], sg_ref[...]

    q = _dot(_rms(cq_ref[...].astype(F32), gq_ref[...]).astype(BF), wuq_ref[...])
    for h in range(MLA_H):
        a = 256 * h
        qm_ref[:, a:a + 128] = (q[:, a:a + 128] * QS_MLA).astype(BF)
        qm_ref[:, a + 128:a + 256] = (rope(q[:, a + 128:a + 256], cm, sm, MLA_ROPE // 4) * QS_MLA).astype(BF)

    n = _rms(ckv_ref[...].astype(F32), gkv_ref[...]).astype(BF)
    kn = _dot(n, wk_ref[...])
    vm_ref[...] = _dot(n, wv_ref[...]).astype(BF)
    kr = rope(krz_ref[...].astype(F32), cm, sm, MLA_ROPE // 4).astype(BF)
    for h in range(MLA_H):
        a = 256 * h
        km_ref[:, a:a + 128] = kn[:, 128 * h:128 * h + 128].astype(BF)
        km_ref[:, a + 128:a + 256] = kr

    for h in range(GQA_H):
        t = _rms(aq_ref[:, 128 * h:128 * h + 128].astype(F32), gaq_ref[...])
        qg_ref[:, 128 * h:128 * h + 128] = (rope(t, cg, sg, GQA_DH // 4) * QS_GQA).astype(BF)
    for h in range(GQA_KV):
        t = _rms(ak_ref[:, 128 * h:128 * h + 128].astype(F32), gak_ref[...])
        kg_ref[:, 128 * h:128 * h + 128] = rope(t, cg, sg, GQA_DH // 4).astype(BF)


def _rope_tables(S):
    pos = np.arange(S)
    rows, cols = pos // GRID_W, pos % GRID_W

    def tab(d, width):
        half = d // 2
        inv = ROPE_THETA ** (-np.arange(0, half, 2, dtype=np.float64) / half)
        ar, ac = rows[:, None] * inv[None], cols[:, None] * inv[None]
        c = np.concatenate([np.cos(ar), np.cos(ar), np.cos(ac), np.cos(ac)], 1)
        s = np.concatenate([-np.sin(ar), np.sin(ar), -np.sin(ac), np.sin(ac)], 1)
        ci = np.ones((CT, d)); si = np.zeros((CT, d))
        c = np.concatenate([c, ci], 0); s = np.concatenate([s, si], 0)
        pad = np.zeros((S + CT, width - d))
        return (jnp.asarray(np.concatenate([c, pad], 1), F32), jnp.asarray(np.concatenate([s, pad], 1), F32))

    return tab(GQA_DH, 128) + tab(MLA_ROPE, 128)


def _prep(proj, gq, gkv, wuq, wk, wv, gaq, gak, tabs, *, S, RL):
    R = proj.shape[0]
    tm = PREP_TM
    nlat = RL // tm
    spb = S // tm
    tab_idx = lambda i: (jnp.where(i < nlat, i % spb, spb), 0)
    col = lambda w, off: pl.BlockSpec((tm, w), lambda i: (i, off // w))
    full = lambda a: pl.BlockSpec(a.shape, lambda i: (0,) * a.ndim)
    tspec = pl.BlockSpec((tm, 128), tab_idx)
    widths = [MLA_H * 256, MLA_H * 256, MLA_H * MLA_DV, GQA_H * GQA_DH, GQA_KV * GQA_DH]
    return pl.pallas_call(
        _prep_kernel,
        grid=(R // tm,),
        in_specs=[col(MLA_QR, OFF_CQ), col(MLA_KVR, OFF_CKV), col(128, OFF_KRZ), col(768, OFF_AQ), col(256, OFF_AK),
                  full(gq), full(gkv), full(wuq), full(wk), full(wv), full(gaq), full(gak),
                  tspec, tspec, tspec, tspec],
        out_specs=[pl.BlockSpec((tm, w), lambda i: (i, 0)) for w in widths],
        out_shape=[jax.ShapeDtypeStruct((R, w), BF) for w in widths],
        compiler_params=_cp(("arbitrary",)),
        name="qk_prep",
    )(proj, proj, proj, proj, proj, gq, gkv, wuq, wk, wv, gaq, gak, *tabs)


GLA_TM = 256


def _gla_kernel(qf_ref, kf_ref, vf_ref, zf_ref, qb_ref, kb_ref, vb_ref, zb_ref, wg_ref, bg_ref, tri_ref,
                of_ref, ob_ref, st_scr):
    @pl.when(pl.program_id(1) == 0)
    def _():
        st_scr[...] = jnp.zeros_like(st_scr)

    dirs = ((qf_ref, kf_ref, vf_ref, zf_ref, of_ref), (qb_ref, kb_ref, vb_ref, zb_ref, ob_ref))
    las, sts = [], []
    for d, (_, _, _, z_ref, _) in enumerate(dirs):
        z = _dot(z_ref[...], wg_ref[d]) + bg_ref[d]
        las.append((jnp.minimum(z, 0.0) - jnp.log(1.0 + jnp.exp(-jnp.abs(z)))) * (1.0 / GLA_TAU))
        sts.append([st_scr[d, h] for h in range(GLA_H)])
    C = GLA_CHUNK
    nc = GLA_TM // C
    for cc in range(nc):
        for d, (q_ref, k_ref, v_ref, _, o_ref) in enumerate(dirs):
            tri = tri_ref[d]
            keep = tri > 0
            r0 = (cc if d == 0 else nc - 1 - cc) * C
            la = las[d][r0:r0 + C, :]
            h1 = la.astype(BF)
            r1 = la - h1.astype(F32)
            h2 = r1.astype(BF)
            h3 = (r1 - h2.astype(F32)).astype(BF)
            b = _dot(tri, h1) + _dot(tri, h2) + _dot(tri, h3)
            tot = jnp.sum(la, axis=0, keepdims=True)
            q = q_ref[r0:r0 + C, :].astype(F32)
            k = k_ref[r0:r0 + C, :].astype(F32)
            qd = (q * (GLA_DK ** -0.5) * jnp.exp(b)).astype(BF)
            ki = (k * jnp.exp(-b)).astype(BF)
            ke = (k * jnp.exp(tot - b)).astype(BF)
            dec = jnp.exp(tot)
            v = v_ref[r0:r0 + C, :]
            for h in range(GLA_H):
                sl = slice(GLA_DK * h, GLA_DK * (h + 1))
                sv = slice(GLA_DV * h, GLA_DV * (h + 1))
                att = jnp.where(keep, _dot_nt(qd[:, sl], ki[:, sl]), 0.0).astype(BF)
                st = sts[d][h]
                o_ref[r0:r0 + C, sv] = _dot(att, v[:, sv]) + _dot_nt(qd[:, sl], st.astype(BF))
                sts[d][h] = st * dec[:, sl] + _dot_tn(v[:, sv], ke[:, sl])
    for d in range(2):
        for h in range(GLA_H):
            st_scr[d, h] = sts[d][h]


def _gla(proj, wg, bg, tri, *, B, S, RL):
    R = proj.shape[0]
    tm = GLA_TM
    nj = S // tm + 1
    lat0 = lambda b: b * (S // tm)

    def rb(d):
        def f(b, j):
            jl = (j - 1) if d == 0 else (S // tm - j)
            return jnp.where(j == 0, RL // tm + b, lat0(b) + jl)
        return f

    col = lambda d, w, off: pl.BlockSpec((tm, w), lambda b, j: (rb(d)(b, j), off // w))
    full = lambda a: pl.BlockSpec(a.shape, lambda b, j: (0,) * a.ndim)
    ins = lambda d: [col(d, 256, OFF_GQ), col(d, 256, OFF_GK), col(d, 512, OFF_GV), col(d, 128, OFF_KRZ)]
    out = lambda d: pl.BlockSpec((tm, GLA_H * GLA_DV), lambda b, j: (rb(d)(b, j), 0))
    return pl.pallas_call(
        _gla_kernel,
        grid=(B, nj),
        in_specs=ins(0) + ins(1) + [full(wg), full(bg), full(tri)],
        out_specs=[out(0), out(1)],
        out_shape=[jax.ShapeDtypeStruct((R, GLA_H * GLA_DV), F32)] * 2,
        scratch_shapes=[pltpu.VMEM((2, GLA_H, GLA_DV, GLA_DK), F32)],
        compiler_params=_cp(("arbitrary", "arbitrary")),
        name="gla",
    )(*([proj] * 8), wg, bg, tri)


def _flash_kernel(*refs, nseg, seg_lens, tq, dv, ck):
    q_ref = refs[0]
    k_refs = refs[1:1 + nseg]
    v_refs = refs[1 + nseg:1 + 2 * nseg]
    o_ref = refs[1 + 2 * nseg]
    s_scr = refs[2 + 2 * nseg]
    q = q_ref[...]
    m = jnp.full((tq, 128), -jnp.inf, F32)
    chunks = []
    off = 0
    for si in range(nseg):
        for c0 in range(0, seg_lens[si], ck):
            cl = min(ck, seg_lens[si] - c0)
            s = _dot_nt(q, k_refs[si][c0:c0 + cl, :])
            s_scr[:, off:off + cl] = s
            for l0 in range(0, cl, 128):
                m = jnp.maximum(m, s[:, l0:l0 + 128])
            chunks.append((si, c0, cl, off))
            off += cl
    mrow = jnp.max(m, axis=1, keepdims=True)
    l = jnp.zeros((tq, 128), F32)
    acc = jnp.zeros((tq, dv), F32)
    for si, c0, cl, off in chunks:
        p = jnp.exp2(s_scr[:, off:off + cl] - mrow)
        for l0 in range(0, cl, 128):
            l = l + p[:, l0:l0 + 128]
        acc = acc + _dot(p.astype(BF), v_refs[si][c0:c0 + cl, :])
    o_ref[...] = (acc / jnp.sum(l, axis=1, keepdims=True)).astype(o_ref.dtype)


def _attention(qa, ka, va, *, B, S, RL, hq, group, dk, dv, vbase, ctx_queries, name):
    cb = RL // CT
    kh = lambda h: h // group
    if ctx_queries:
        tq, nq, n_out = CT, 1, B * CT
        q_spec = pl.BlockSpec((tq, dk), lambda b, h, i: (cb + b, h))
        o_spec = pl.BlockSpec((tq, dv), lambda b, h, i: (b, h))
        k_specs = [pl.BlockSpec((CT, dk), lambda b, h, i: (cb + b, kh(h)))]
        v_specs = [pl.BlockSpec((CT, dv), lambda b, h, i: (cb + b, vbase + kh(h)))]
        seg_lens = (CT,)
    else:
        tq, nq, n_out = 256, S // 256, RL
        q_spec = pl.BlockSpec((tq, dk), lambda b, h, i: (b * nq + i, h))
        o_spec = pl.BlockSpec((tq, dv), lambda b, h, i: (b * nq + i, h))
        k_specs = [pl.BlockSpec((CT, dk), lambda b, h, i: (cb + b, kh(h))),
                   pl.BlockSpec((S, dk), lambda b, h, i: (b, kh(h)))]
        v_specs = [pl.BlockSpec((CT, dv), lambda b, h, i: (cb + b, vbase + kh(h))),
                   pl.BlockSpec((S, dv), lambda b, h, i: (b, vbase + kh(h)))]
        seg_lens = (CT, S)
    nseg = len(seg_lens)
    kern = functools.partial(_flash_kernel, nseg=nseg, seg_lens=seg_lens, tq=tq, dv=dv, ck=512)
    return pl.pallas_call(
        kern,
        grid=(B, hq, nq),
        in_specs=[q_spec] + k_specs + v_specs,
        out_specs=o_spec,
        out_shape=jax.ShapeDtypeStruct((n_out, hq * dv), BF),
        scratch_shapes=[pltpu.VMEM((tq, sum(seg_lens)), F32)],
        compiler_params=_cp(("arbitrary", "arbitrary", "arbitrary")),
        name=name,
    )(qa, *([ka] * nseg), *([va] * nseg))


WOUT_TM = 512
COUNT_BITS = 20


def _route(logit, rb):
    T = logit.shape[1]
    gsz = N_EXP // N_GRP
    scores = _sigmoid(logit)
    sel = scores + rb
    ninf = jnp.float32(-jnp.inf)
    sel3 = sel.reshape(N_GRP, gsz, T)
    idx3 = lax.broadcasted_iota(I32, (N_GRP, gsz, T), 1)
    m1 = jnp.max(sel3, axis=1, keepdims=True)
    first = jnp.min(jnp.where(sel3 == m1, idx3, gsz), axis=1, keepdims=True)
    m2 = jnp.max(jnp.where(idx3 == first, ninf, sel3), axis=1, keepdims=True)
    gs = (m1 + m2).reshape(N_GRP, T)
    gi = lax.broadcasted_iota(I32, (N_GRP, T), 0)
    grank = jnp.zeros((N_GRP, T), I32)
    for j in range(N_GRP):
        row = gs[j:j + 1, :]
        grank = grank + ((row > gs) | ((row == gs) & (j < gi))).astype(I32)
    gkeep = (grank < TOPK_GRP).reshape(N_GRP, 1, T)
    selm = jnp.where(jnp.broadcast_to(gkeep, (N_GRP, gsz, T)), sel3, ninf).reshape(N_EXP, T)
    ei = lax.broadcasted_iota(I32, (N_EXP, T), 0)
    erank = jnp.zeros((N_EXP, T), I32)
    for j in range(N_EXP):
        row = selm[j:j + 1, :]
        erank = erank + ((row > selm) | ((row == selm) & (j < ei))).astype(I32)
    chosen = erank < TOP_K
    w = jnp.where(chosen, scores, 0.0)
    wd = w / jnp.sum(w, axis=0, keepdims=True) * ROUTED_SCALE
    return wd, chosen


def _count_to_int(x, nbits):
    out = jnp.zeros(x.shape, I32)
    for b in reversed(range(nbits)):
        ge = x >= float(1 << b)
        out = out + jnp.where(ge, 1 << b, 0)
        x = x - jnp.where(ge, float(1 << b), 0.0)
    return out


def _store_tile_rows(ref, packed, tm):
    for cblk in range(HALF // 128):
        ref[pl.ds(cblk, tm, stride=8), :] = packed[:, 128 * cblk:128 * (cblk + 1)]


def _load_tile_rows(ref, tm):
    los, his = [], []
    for cblk in range(HALF // 128):
        lo, hi = _unpack_halves(ref[pl.ds(cblk, tm, stride=8), :])
        los.append(lo.astype(BF))
        his.append(hi.astype(BF))
    return jnp.concatenate(los, axis=1), jnp.concatenate(his, axis=1)


def _wout_kernel(of_ref, ob_ref, gr_ref, mla_ref, gqa_ref, x_ref, w_ref, gout_ref, gpm_ref, gtm_ref,
                 gpf_ref, shf_ref, scf_ref, rwt_ref, rb_ref, ust_ref, lst_ref,
                 xn_ref, hfp_ref, r8_ref, e8_ref, w8_ref, cnt_ref, cnt_scr):
    @pl.when(pl.program_id(0) == 0)
    def _():
        cnt_scr[...] = jnp.zeros_like(cnt_scr)

    og = of_ref[...] + ob_ref[...]
    gr = gr_ref[...].astype(F32)
    parts = []
    for h in range(GLA_H):
        sv = slice(GLA_DV * h, GLA_DV * (h + 1))
        g = gr[:, sv]
        parts.append((_rms(og[:, sv], gout_ref[...]) * (g * _sigmoid(g))).astype(BF))
    gla = jnp.concatenate(parts, axis=1)
    n_g, n_m = GLA_H * GLA_DV, MLA_H * MLA_DV
    y = (_dot(gla, w_ref[0:n_g, :]) + _dot(mla_ref[...], w_ref[n_g:n_g + n_m, :])
         + _dot(gqa_ref[...], w_ref[n_g + n_m:, :]))
    xn = x_ref[...] + gtm_ref[0] * _rms(y, gpm_ref[...])
    xn_ref[...] = xn
    hf = _rms(xn, gpf_ref[...]) * (1.0 + scf_ref[0]) + shf_ref[0]
    _store_tile_rows(hfp_ref, _pack_halves(hf), WOUT_TM)
    hh, hl = _split2(hf)
    rh, rl = _split2(rwt_ref[...])
    logit = _dot_nt(rh, hh) + _dot_nt(rl, hh) + _dot_nt(rh, hl)
    wd, chosen = _route(logit, rb_ref[...])

    cb = jnp.where(chosen, 1.0, 0.0)
    before = _dot(cb.astype(BF), ust_ref[...])
    rank = cnt_scr[:, 0:1] + before
    slot = _dot(lst_ref[...], cb.astype(BF))
    eid = lax.broadcasted_iota(I32, chosen.shape, 0).astype(F32)
    r8, e8, w8 = [], [], []
    for k in range(TOP_K):
        pick = chosen & (slot == float(k))
        r8.append(jnp.sum(jnp.where(pick, rank, 0.0), axis=0, keepdims=True))
        e8.append(jnp.sum(jnp.where(pick, eid, 0.0), axis=0, keepdims=True))
        w8.append(jnp.sum(jnp.where(pick, wd, 0.0), axis=0, keepdims=True))
    r8_ref[...] = _count_to_int(jnp.concatenate(r8, axis=0), COUNT_BITS)
    e8_ref[...] = _count_to_int(jnp.concatenate(e8, axis=0), 6)
    w8_ref[...] = jnp.concatenate(w8, axis=0)
    cnt_scr[...] = cnt_scr[...] + jnp.sum(cb, axis=1, keepdims=True)
    cnt_ref[...] = _count_to_int(cnt_scr[...], COUNT_BITS)


def _wout(gla_o, proj, mla_o, gqa_o, xr, w, gout, gpm, gtm, gpf, shf, scf, rwt, rb, *, n_rows, tiles_per_batch, nb):
    tm = WOUT_TM
    mod_idx = lambda i: (jnp.minimum(i // tiles_per_batch, nb), 0, 0)
    row = lambda w_: pl.BlockSpec((tm, w_), lambda i: (i, 0))
    full = lambda a: pl.BlockSpec(a.shape, lambda i: (0,) * a.ndim)
    mspec = pl.BlockSpec((1, 1, D), mod_idx)
    it = np.arange(tm)
    ust = jnp.asarray(it[:, None] < it[None, :], BF)
    ie = np.arange(N_EXP)
    lst = jnp.asarray(ie[None, :] < ie[:, None], BF)
    k8 = pl.BlockSpec((TOP_K, tm), lambda i: (0, i))
    return pl.pallas_call(
        _wout_kernel,
        grid=(n_rows // tm,),
        in_specs=[row(512), row(512),
                  pl.BlockSpec((tm, 512), lambda i: (i, OFF_GR // 512)),
                  row(768), row(768), row(D), full(w), full(gout), full(gpm), mspec,
                  full(gpf), mspec, mspec, full(rwt), full(rb), full(ust), full(lst)],
        out_specs=[row(D), pl.BlockSpec((tm * 8, 128), lambda i: (i, 0)), k8, k8, k8,
                   pl.BlockSpec((N_EXP, 128), lambda i: (0, 0))],
        out_shape=[jax.ShapeDtypeStruct((n_rows, D), F32), jax.ShapeDtypeStruct((n_rows * 8, 128), PACKED),
                   jax.ShapeDtypeStruct((TOP_K, n_rows), I32), jax.ShapeDtypeStruct((TOP_K, n_rows), I32),
                   jax.ShapeDtypeStruct((TOP_K, n_rows), F32), jax.ShapeDtypeStruct((N_EXP, 128), I32)],
        scratch_shapes=[pltpu.VMEM((N_EXP, 128), F32)],
        compiler_params=_cp(("arbitrary",)),
        name="wout_route",
    )(gla_o[0], gla_o[1], proj, mla_o, gqa_o, xr, w, gout, gpm, gtm, gpf, shf, scf, rwt, rb, ust, lst)


MOE_TM = 512
DISP_TM = 256
COMB_TM = 128


def _tile_row(ref, r):
    return ref.at[pl.ds(pl.multiple_of(r * 8, 8), 8), :]


def _dispatch_kernel(cnt_ref, pst_ref, ntl_ref, nu_ref, pos_ref, hfp_ref, xs_ref, zero_scr, sem, zsem):
    i = pl.program_id(0)
    tm = DISP_TM
    tile_rows = MOE_TM * 8
    n_tiles = xs_ref.shape[0] // tile_rows

    def issue(t, carry):
        for k in range(TOP_K):
            p = pos_ref[0, 0, k * tm + t]
            pltpu.make_async_copy(_tile_row(hfp_ref, t), _tile_row(xs_ref, p), sem).start(priority=k % 2)
        return carry

    lax.fori_loop(0, tm, issue, 0, unroll=2)

    @pl.when(i == 0)
    def _():
        zero_scr[...] = _packed_zeros(zero_scr.shape)

        def per_expert(e, carry):
            cnt = cnt_ref[e]
            padlen = ntl_ref[e] * MOE_TM - cnt
            off = pst_ref[e] + cnt

            pieces = []
            bit = MOE_TM // 2
            while bit >= 1:
                take = padlen & bit
                dst = xs_ref.at[pl.ds(pl.multiple_of(off * 8, 8), bit * 8), :]
                pieces.append((take != 0, pltpu.make_async_copy(zero_scr.at[pl.ds(0, bit * 8), :], dst, zsem)))
                off = off + take
                bit //= 2
            for go, cp in pieces:
                pl.when(go)(cp.start)
            for go, cp in pieces:
                pl.when(go)(cp.wait)
            return carry

        lax.fori_loop(0, N_EXP, per_expert, 0)

        def ztile(t):
            dst = xs_ref.at[pl.ds(pl.multiple_of(t * tile_rows, tile_rows), tile_rows), :]
            return pltpu.make_async_copy(zero_scr, dst, zsem)

        lax.fori_loop(nu_ref[0], n_tiles, lambda t, cr: (ztile(t).start(), cr)[1], 0)
        lax.fori_loop(nu_ref[0], n_tiles, lambda t, cr: (ztile(t).wait(), cr)[1], 0)

    for k in range(TOP_K):
        pltpu.make_async_copy(hfp_ref, xs_ref.at[pl.ds(0, tm * 8), :], sem).wait()


def _dispatch(cnt, pst, ntl, n_used, pos_t, hfp, *, n_rows_sorted):
    n = hfp.shape[0] // 8
    tm = DISP_TM
    grid_spec = pltpu.PrefetchScalarGridSpec(
        num_scalar_prefetch=4,
        grid=(n // tm,),
        in_specs=[pl.BlockSpec((1, 1, tm * TOP_K), lambda i, *_: (i, 0, 0), memory_space=pltpu.SMEM),
                  pl.BlockSpec((tm * 8, 128), lambda i, *_: (i, 0))],
        out_specs=pl.BlockSpec(memory_space=pl.ANY),
        scratch_shapes=[pltpu.VMEM((MOE_TM * 8, 128), PACKED), pltpu.SemaphoreType.DMA, pltpu.SemaphoreType.DMA],
    )
    return pl.pallas_call(
        _dispatch_kernel,
        grid_spec=grid_spec,
        out_shape=jax.ShapeDtypeStruct((n_rows_sorted * 8, 128), PACKED),
        compiler_params=_cp(("arbitrary",)),
        name="moe_dispatch",
    )(cnt, pst, ntl, n_used, pos_t, hfp)


def _gmm_kernel(te_ref, nu_ref, x_ref, w1_ref, w3_ref, w2_ref, y_ref, w1b, w3b, w2b):
    i = pl.program_id(0)
    used = i < nu_ref[0]

    @pl.when(used)
    def _():
        changed = jnp.logical_or(i == 0, te_ref[jnp.maximum(i - 1, 0)] != te_ref[i])

        @pl.when(changed)
        def _():
            w1b[...] = w1_ref[0].astype(BF)
            w3b[...] = w3_ref[0].astype(BF)
            w2b[...] = w2_ref[0].astype(BF)

        xa, xb = _load_tile_rows(x_ref, MOE_TM)
        h1 = _dot(xa, w1b[0:HALF, :]) + _dot(xb, w1b[HALF:, :])
        h3 = _dot(xa, w3b[0:HALF, :]) + _dot(xb, w3b[HALF:, :])
        h = (h1 * _sigmoid(h1) * h3).astype(BF)
        _store_tile_rows(y_ref, _pack_halves(_dot(h, w2b[...])), MOE_TM)

    @pl.when(jnp.logical_not(used))
    def _():
        y_ref[...] = _packed_zeros(y_ref.shape)


def _gmm(tile_e, n_used, xs, w1, w3, w2, layer):
    tm = MOE_TM
    n_tiles = xs.shape[0] // (tm * 8)
    last = lambda i, nu: jnp.minimum(i, nu[0] - 1)
    wsel = lambda i, te, nu: (layer * N_EXP + te[last(i, nu)], 0, 0)
    grid_spec = pltpu.PrefetchScalarGridSpec(
        num_scalar_prefetch=2,
        grid=(n_tiles,),
        in_specs=[pl.BlockSpec((tm * 8, 128), lambda i, te, nu: (last(i, nu), 0)),
                  pl.BlockSpec((1, D, D_EXP), wsel),
                  pl.BlockSpec((1, D, D_EXP), wsel),
                  pl.BlockSpec((1, D_EXP, D), wsel)],
        out_specs=pl.BlockSpec((tm * 8, 128), lambda i, te, nu: (i, 0)),
        scratch_shapes=[pltpu.VMEM((D, D_EXP), BF), pltpu.VMEM((D, D_EXP), BF), pltpu.VMEM((D_EXP, D), BF)],
    )
    return pl.pallas_call(
        _gmm_kernel,
        grid_spec=grid_spec,
        out_shape=jax.ShapeDtypeStruct(xs.shape, PACKED),
        compiler_params=_cp(("arbitrary",)),
        name="moe_experts",
    )(tile_e, n_used, xs, w1, w3, w2)


def _combine_kernel(pos_ref, posn_ref, w8_ref, hfp_ref, x_ref, s1_ref, s3_ref, s2_ref, gpost_ref, gtf_ref, ys_ref,
                    o_ref, ybuf_a, ybuf_b, sem_a, sem_b):
    tm = COMB_TM
    i = pl.program_id(0)
    last = pl.num_programs(0) - 1

    def gather(pref, buf, sem):
        for t in range(tm):
            for k in range(TOP_K):
                p = pref[0, 0, k * tm + t]
                pltpu.make_async_copy(_tile_row(ys_ref, p), buf.at[k, pl.ds(t * 8, 8), :], sem).start(priority=k % 2)

    def drain(buf, sem):
        for k in range(TOP_K):
            pltpu.make_async_copy(ys_ref.at[pl.ds(0, tm * 8), :], buf.at[k], sem).wait()

    @pl.when(i == 0)
    def _():
        gather(pos_ref, ybuf_a, sem_a)

    def body(cur, sem_c, nxt, sem_n):
        gather(posn_ref, nxt, sem_n)
        xa, xb = _load_tile_rows(hfp_ref, tm)
        h1 = _dot(xa, s1_ref[0:HALF, :]) + _dot(xb, s1_ref[HALF:, :])
        h3 = _dot(xa, s3_ref[0:HALF, :]) + _dot(xb, s3_ref[HALF:, :])
        sh = _dot((h1 * _sigmoid(h1) * h3).astype(BF), s2_ref[...])
        drain(cur, sem_c)
        w8 = w8_ref[...]
        wk = [w8[:, k:k + 1] for k in range(TOP_K)]
        los, his = [], []
        for cblk in range(HALF // 128):
            a_lo = sh[:, 128 * cblk:128 * (cblk + 1)]
            a_hi = sh[:, HALF + 128 * cblk:HALF + 128 * (cblk + 1)]
            for k in range(TOP_K):
                lo, hi = _unpack_halves(cur[k, pl.ds(cblk, tm, stride=8), :])
                a_lo = a_lo + wk[k] * lo
                a_hi = a_hi + wk[k] * hi
            los.append(a_lo)
            his.append(a_hi)
        yf = jnp.concatenate(los + his, axis=1)
        o_ref[...] = x_ref[...] + gtf_ref[0] * _rms(yf, gpost_ref[...])

        @pl.when(i == last)
        def _():
            drain(nxt, sem_n)

    @pl.when(i % 2 == 0)
    def _():
        body(ybuf_a, sem_a, ybuf_b, sem_b)

    @pl.when(i % 2 == 1)
    def _():
        body(ybuf_b, sem_b, ybuf_a, sem_a)


def _combine(pos_t, w8, hfp, xn, s1, s3, s2, gpost, gtf, ys, *, tiles_per_batch, nb):
    n = hfp.shape[0] // 8
    tm = COMB_TM
    nt = n // tm
    mod_idx = lambda i: (jnp.minimum(i // tiles_per_batch, nb), 0, 0)
    full = lambda a: pl.BlockSpec(a.shape, lambda i: (0,) * a.ndim)
    buf = pltpu.VMEM((TOP_K, tm * 8, 128), PACKED)
    return pl.pallas_call(
        _combine_kernel,
        grid=(nt,),
        in_specs=[pl.BlockSpec((1, 1, tm * TOP_K), lambda i: (i, 0, 0), memory_space=pltpu.SMEM),
                  pl.BlockSpec((1, 1, tm * TOP_K), lambda i: (jnp.minimum(i + 1, nt - 1), 0, 0),
                               memory_space=pltpu.SMEM),
                  pl.BlockSpec((tm, TOP_K), lambda i: (i, 0)),
                  pl.BlockSpec((tm * 8, 128), lambda i: (i, 0)),
                  pl.BlockSpec((tm, D), lambda i: (i, 0)),
                  full(s1), full(s3), full(s2), full(gpost),
                  pl.BlockSpec((1, 1, D), mod_idx),
                  pl.BlockSpec(memory_space=pl.ANY)],
        out_specs=pl.BlockSpec((tm, D), lambda i: (i, 0)),
        out_shape=jax.ShapeDtypeStruct((n, D), F32),
        scratch_shapes=[buf, buf, pltpu.SemaphoreType.DMA, pltpu.SemaphoreType.DMA],
        compiler_params=_cp(("arbitrary",)),
        name="moe_combine",
    )(pos_t, pos_t, w8, hfp, xn, s1, s3, s2, gpost, gtf, ys)


def _dispatch_tables(counts, r8, e8):
    n = r8.shape[1]
    ntl = (counts + MOE_TM - 1) // MOE_TM
    tend = jnp.cumsum(ntl)
    pst = (tend - ntl) * MOE_TM
    onehot = e8[None] == jnp.arange(N_EXP, dtype=I32)[:, None, None]
    pos = r8 + jnp.sum(jnp.where(onehot, pst[:, None, None], 0), axis=0)
    n_tiles = -(-(n * TOP_K) // MOE_TM) + N_EXP
    tile_e = jnp.sum((jnp.arange(n_tiles, dtype=I32)[:, None] >= tend[None, :]).astype(I32), axis=1)
    tile_e = jnp.minimum(tile_e, N_EXP - 1)
    return pst.astype(I32), ntl.astype(I32), pos.astype(I32), tile_e.astype(I32), tend[-1:].astype(I32), n_tiles


def _pos_blocks(pos, tm):
    k, n = pos.shape
    return pos.reshape(k, n // tm, tm).transpose(1, 0, 2).reshape(n // tm, 1, k * tm)


def _inproj_weight(w_in):
    offs = np.cumsum((0, 256, 256, 512, 512, 16, 16, 384, 512, 64, 768, 256, 256))
    p = [w_in[:, offs[i]:offs[i + 1]] for i in range(12)]
    gq, gk, gv, gr, gzf, gzb, cq, ckv, kr, aq, ak, av = p
    krz = jnp.concatenate([kr, gzf, gzb, jnp.zeros((D, 32), w_in.dtype)], axis=1)
    return jnp.concatenate([gv, gr, ckv, aq, cq, krz, gq, gk, ak, av], axis=1).astype(BF)


def _mla_weights(w_uq, w_ukv):
    wq = w_uq.reshape(MLA_QR, MLA_H, MLA_NOPE + MLA_ROPE)
    wq = jnp.concatenate([wq, jnp.zeros((MLA_QR, MLA_H, 256 - MLA_NOPE - MLA_ROPE), wq.dtype)], axis=2)
    wkv = w_ukv.reshape(MLA_KVR, MLA_H, MLA_NOPE + MLA_DV)
    wk = wkv[:, :, :MLA_NOPE].reshape(MLA_KVR, MLA_H * MLA_NOPE)
    wv = wkv[:, :, MLA_NOPE:].reshape(MLA_KVR, MLA_H * MLA_DV)
    return wq.reshape(MLA_QR, MLA_H * 256).astype(BF), wk.astype(BF), wv.astype(BF)


def _gla_gate_weights(wf, bf, wb, bb):
    z = jnp.zeros((128, GLA_H * GLA_DK), F32)
    wg = jnp.stack([z.at[64:80].set(wf), z.at[80:96].set(wb)]).astype(BF)
    bg = jnp.stack([bf, bb]).reshape(2, 1, GLA_H * GLA_DK)
    return wg, bg


def kernel(x, c, ctx, c_ctx, w_mod, b_mod, g_pre_mix, g_post_mix, g_pre_ffn, g_post_ffn, w_in, gla_wg2_f, gla_bg_f, gla_wg2_b, gla_bg_b, gla_g_out, mla_g_q, mla_w_uq, mla_g_kv, mla_w_ukv, gqa_g_q, gqa_g_k, w_out, router_w, router_b, exp_w1, exp_w3, exp_w2, sh_w1, sh_w3, sh_w2):
    B, S, _ = x.shape
    depth = w_mod.shape[0]
    assert ctx.shape[1] == CT and S % 256 == 0 and B < 8 and B * (S + CT) < (1 << COUNT_BITS)
    RL, RC = B * S, B * CT
    R = RL + RC

    xr = jnp.concatenate([x.reshape(RL, D), ctx.reshape(RC, D)], axis=0)
    cvecs = jnp.concatenate([c, c_ctx[None], jnp.zeros((8 - B - 1, D), F32)], axis=0)
    mods = _modulation(cvecs, w_mod, b_mod)
    tabs = _rope_tables(S)
    ii = np.arange(GLA_CHUNK)
    tri = jnp.asarray(np.stack([ii[:, None] >= ii[None, :], ii[:, None] <= ii[None, :]]), BF)

    in_tm = 1024 if (S % 1024 == 0 and RC % 1024 == 0) else 256
    row2 = lambda v: v.reshape(1, -1)
    ew1 = exp_w1.reshape(depth * N_EXP, D, D_EXP)
    ew3 = exp_w3.reshape(depth * N_EXP, D, D_EXP)
    ew2 = exp_w2.reshape(depth * N_EXP, D_EXP, D)

    for l in range(depth):
        want_ctx = l < depth - 1
        m = mods[l].reshape(8, 6, 1, D)
        sh_m, sc_m, gt_m, sh_f, sc_f, gt_f = [m[:, i] for i in range(6)]

        proj = _inproj(xr, row2(g_pre_mix[l]), sh_m, sc_m, _inproj_weight(w_in[l]),
                       tm=in_tm, tiles_per_batch=S // in_tm, nb=B)
        wuq, wk, wv = _mla_weights(mla_w_uq[l], mla_w_ukv[l])
        qm, km, vm, qg, kg = _prep(proj, row2(mla_g_q[l]), row2(mla_g_kv[l]), wuq, wk, wv,
                                   row2(gqa_g_q[l]), row2(gqa_g_k[l]), tabs, S=S, RL=RL)
        wg, bg = _gla_gate_weights(gla_wg2_f[l], gla_bg_f[l], gla_wg2_b[l], gla_bg_b[l])
        gla_o = _gla(proj, wg, bg, tri, B=B, S=S, RL=RL)

        att = functools.partial(_attention, B=B, S=S, RL=RL)
        mla = functools.partial(att, qm, km, vm, hq=MLA_H, group=1, dk=256, dv=MLA_DV, vbase=0)
        gqa = functools.partial(att, qg, kg, proj, hq=GQA_H, group=GQA_H // GQA_KV, dk=GQA_DH, dv=GQA_DH,
                                vbase=OFF_AV // GQA_DH)
        mla_o = mla(ctx_queries=False, name="mla_lat")
        gqa_o = gqa(ctx_queries=False, name="gqa_lat")
        if want_ctx:
            mla_o = jnp.concatenate([mla_o, mla(ctx_queries=True, name="mla_ctx")], axis=0)
            gqa_o = jnp.concatenate([gqa_o, gqa(ctx_queries=True, name="gqa_ctx")], axis=0)
        n_rows = R if want_ctx else RL

        xn, hfp, r8, e8, w8, cnt = _wout(gla_o, proj, mla_o, gqa_o, xr, w_out[l].astype(BF), row2(gla_g_out[l]),
                                         row2(g_post_mix[l]), gt_m, row2(g_pre_ffn[l]), sh_f, sc_f,
                                         router_w[l].T, router_b[l].reshape(N_EXP, 1),
                                         n_rows=n_rows, tiles_per_batch=S // WOUT_TM, nb=B)

        counts = cnt[:, 0]
        pst, ntl, pos, tile_e, n_used, n_tiles = _dispatch_tables(counts, r8, e8)
        xs = _dispatch(counts, pst, ntl, n_used, _pos_blocks(pos, DISP_TM), hfp, n_rows_sorted=n_tiles * MOE_TM)
        ys = _gmm(tile_e, n_used, xs, ew1, ew3, ew2, l)
        xo = _combine(_pos_blocks(pos, COMB_TM), w8.T, hfp, xn,
                      sh_w1[l].astype(BF), sh_w3[l].astype(BF), sh_w2[l].astype(BF),
                      row2(g_post_ffn[l]), gt_f, ys, tiles_per_batch=S // COMB_TM, nb=B)
        xr = xo
    return xr[:RL].reshape(B, S, D)
```

```python
import functools
import math

import numpy as np
import jax
import jax.numpy as jnp
from jax import lax
from jax.experimental import pallas as pl
from jax.experimental.pallas import tpu as pltpu

F32 = jnp.float32
BF = jnp.bfloat16
PACKED = jnp.uint32
I32 = jnp.int32

D = 2048
GRID_W = 64
CT = 256
ROPE_THETA = 10000.0
EPS = 1e-6
GLA_H, GLA_DK, GLA_DV, GLA_RANK, GLA_TAU, GLA_CHUNK = 4, 64, 128, 16, 16.0, 64
MLA_H, MLA_QR, MLA_KVR, MLA_NOPE, MLA_ROPE, MLA_DV = 6, 384, 512, 128, 64, 128
GQA_H, GQA_KV, GQA_DH = 6, 2, 128
N_EXP, TOP_K, N_GRP, TOPK_GRP, D_EXP, D_SH = 64, 8, 8, 4, 512, 512
ROUTED_SCALE = 2.5
LOG2E = math.log2(math.e)
HALF = D // 2

OFF_GV, OFF_GR, OFF_CKV, OFF_AQ, OFF_CQ, OFF_KRZ, OFF_GQ, OFF_GK, OFF_AK, OFF_AV = (
    0, 512, 1024, 1536, 2304, 2688, 2816, 3072, 3328, 3584)
PROJ_W = 3840
VMEM_LIMIT = 56 * 1024 * 1024


def _cp(sem, vmem=None):
    return pltpu.CompilerParams(dimension_semantics=sem, vmem_limit_bytes=vmem or VMEM_LIMIT)


def _dot(a, b):
    return jnp.dot(a, b, preferred_element_type=F32)


def _dot_nt(a, b):
    return lax.dot_general(a, b, (((1,), (1,)), ((), ())), preferred_element_type=F32)


def _dot_tn(a, b):
    return lax.dot_general(a, b, (((0,), (0,)), ((), ())), preferred_element_type=F32)


def _sigmoid(x):
    return 1.0 / (1.0 + jnp.exp(-x))


def _rms(x, g):
    return x * lax.rsqrt(jnp.mean(x * x, axis=-1, keepdims=True) + EPS) * g


def _split2(x):
    hi = x.astype(BF)
    return hi, (x - hi.astype(F32)).astype(BF)


def _pack_halves(y):
    return pltpu.pack_elementwise([y[:, :HALF], y[:, HALF:]], packed_dtype=BF)


def _packed_zeros(shape):
    z = jnp.zeros(shape, F32)
    return pltpu.pack_elementwise([z, z], packed_dtype=BF)


def _unpack_halves(w):
    lo = pltpu.unpack_elementwise(w, index=0, packed_dtype=BF, unpacked_dtype=F32)
    hi = pltpu.unpack_elementwise(w, index=1, packed_dtype=BF, unpacked_dtype=F32)
    return lo, hi


def _mod_kernel(a_ref, w_ref, b_ref, o_ref):
    a = a_ref[...]
    a = a * _sigmoid(a)
    ah, al = _split2(a)
    wh, wl = _split2(w_ref[0])
    o_ref[0] = _dot(ah, wh) + _dot(al, wh) + _dot(ah, wl) + b_ref[0]


def _modulation(cvecs, w_mod, b_mod):
    L, _, n = w_mod.shape
    tn = 1024
    return pl.pallas_call(
        _mod_kernel,
        grid=(L, n // tn),
        in_specs=[pl.BlockSpec((8, D), lambda l, j: (0, 0)),
                  pl.BlockSpec((1, D, tn), lambda l, j: (l, 0, j)),
                  pl.BlockSpec((1, 1, tn), lambda l, j: (l, 0, j))],
        out_specs=pl.BlockSpec((1, 8, tn), lambda l, j: (l, 0, j)),
        out_shape=jax.ShapeDtypeStruct((L, 8, n), F32),
        compiler_params=_cp(("arbitrary", "arbitrary")),
        name="modulation",
    )(cvecs, w_mod, b_mod.reshape(L, 1, n))


def _inproj_kernel(x_ref, g_ref, sh_ref, sc_ref, w_ref, o_ref, a_scr):
    @pl.when(pl.program_id(1) == 0)
    def _():
        y = _rms(x_ref[...], g_ref[...])
        a_scr[...] = (y * (1.0 + sc_ref[0]) + sh_ref[0]).astype(BF)

    o_ref[...] = _dot(a_scr[...], w_ref[...]).astype(o_ref.dtype)


def _inproj(xr, g, sh, sc, w, *, tm, tiles_per_batch, nb):
    R = xr.shape[0]
    tn = 768
    mod_idx = lambda i, j: (jnp.minimum(i // tiles_per_batch, nb), 0, 0)
    return pl.pallas_call(
        _inproj_kernel,
        grid=(R // tm, PROJ_W // tn),
        in_specs=[pl.BlockSpec((tm, D), lambda i, j: (i, 0)),
                  pl.BlockSpec((1, D), lambda i, j: (0, 0)),
                  pl.BlockSpec((1, 1, D), mod_idx),
                  pl.BlockSpec((1, 1, D), mod_idx),
                  pl.BlockSpec((D, tn), lambda i, j: (0, j))],
        out_specs=pl.BlockSpec((tm, tn), lambda i, j: (i, j)),
        out_shape=jax.ShapeDtypeStruct((R, PROJ_W), BF),
        scratch_shapes=[pltpu.VMEM((tm, D), BF)],
        compiler_params=_cp(("arbitrary", "arbitrary")),
        name="inproj",
    )(xr, g, sh, sc, w)


PREP_TM = 256
QS_MLA = (MLA_NOPE + MLA_ROPE) ** -0.5 * LOG2E
QS_GQA = GQA_DH ** -0.5 * LOG2E


def _prep_kernel(cq_ref, ckv_ref, krz_ref, aq_ref, ak_ref, gq_ref, gkv_ref, wuq_ref, wk_ref, wv_ref,
                 gaq_ref, gak_ref, cg_ref, sg_ref, cm_ref, sm_ref,
                 qm_ref, km_ref, vm_ref, qg_ref, kg_ref):
    lane = lax.broadcasted_iota(I32, (PREP_TM, 128), 1)

    def rope(t, c, s, half):
        nxt = pltpu.roll(t, 128 - half, 1)
        prv = pltpu.roll(t, half, 1)
        return t * c + jnp.where((lane % (2 * half)) < half, nxt, prv) * s

    cm, sm = cm_ref[...], sm_ref[...]
    cg, sg = cg_ref[...], sg_ref[...]

    q = _dot(_rms(cq_ref[...].astype(F32), gq_ref[...]).astype(BF), wuq_ref[...])
    for h in range(MLA_H):
        a = 256 * h
        qm_ref[:, a:a + 128] = (q[:, a:a + 128] * QS_MLA).astype(BF)
        qm_ref[:, a + 128:a + 256] = (rope(q[:, a + 128:a + 256], cm, sm, MLA_ROPE // 4) * QS_MLA).astype(BF)

    n = _rms(ckv_ref[...].astype(F32), gkv_ref[...]).astype(BF)
    kn = _dot(n, wk_ref[...])
    vm_ref[...] = _dot(n, wv_ref[...]).astype(BF)
    kr = rope(krz_ref[...].astype(F32), cm, sm, MLA_ROPE // 4).astype(BF)
    for h in range(MLA_H):
        a = 256 * h
        km_ref[:, a:a + 128] = kn[:, 128 * h:128 * h + 128].astype(BF)
        km_ref[:, a + 128:a + 256] = kr

    for h in range(GQA_H):
        t = _rms(aq_ref[:, 128 * h:128 * h + 128].astype(F32), gaq_ref[...])
        qg_ref[:, 128 * h:128 * h + 128] = (rope(t, cg, sg, GQA_DH // 4) * QS_GQA).astype(BF)
    for h in range(GQA_KV):
        t = _rms(ak_ref[:, 128 * h:128 * h + 128].astype(F32), gak_ref[...])
        kg_ref[:, 128 * h:128 * h + 128] = rope(t, cg, sg, GQA_DH // 4).astype(BF)


def _rope_tables(S):
    pos = np.arange(S)
    rows, cols = pos // GRID_W, pos % GRID_W

    def tab(d, width):
        half = d // 2
        inv = ROPE_THETA ** (-np.arange(0, half, 2, dtype=np.float64) / half)
        ar, ac = rows[:, None] * inv[None], cols[:, None] * inv[None]
        c = np.concatenate([np.cos(ar), np.cos(ar), np.cos(ac), np.cos(ac)], 1)
        s = np.concatenate([-np.sin(ar), np.sin(ar), -np.sin(ac), np.sin(ac)], 1)
        ci = np.ones((CT, d)); si = np.zeros((CT, d))
        c = np.concatenate([c, ci], 0); s = np.concatenate([s, si], 0)
        pad = np.zeros((S + CT, width - d))
        return (jnp.asarray(np.concatenate([c, pad], 1), F32), jnp.asarray(np.concatenate([s, pad], 1), F32))

    return tab(GQA_DH, 128) + tab(MLA_ROPE, 128)


def _prep(proj, gq, gkv, wuq, wk, wv, gaq, gak, tabs, *, S, RL):
    R = proj.shape[0]
    tm = PREP_TM
    nlat = RL // tm
    spb = S // tm
    tab_idx = lambda i: (jnp.where(i < nlat, i % spb, spb), 0)
    col = lambda w, off: pl.BlockSpec((tm, w), lambda i: (i, off // w))
    full = lambda a: pl.BlockSpec(a.shape, lambda i: (0,) * a.ndim)
    tspec = pl.BlockSpec((tm, 128), tab_idx)
    widths = [MLA_H * 256, MLA_H * 256, MLA_H * MLA_DV, GQA_H * GQA_DH, GQA_KV * GQA_DH]
    return pl.pallas_call(
        _prep_kernel,
        grid=(R // tm,),
        in_specs=[col(MLA_QR, OFF_CQ), col(MLA_KVR, OFF_CKV), col(128, OFF_KRZ), col(768, OFF_AQ), col(256, OFF_AK),
                  full(gq), full(gkv), full(wuq), full(wk), full(wv), full(gaq), full(gak),
                  tspec, tspec, tspec, tspec],
        out_specs=[pl.BlockSpec((tm, w), lambda i: (i, 0)) for w in widths],
        out_shape=[jax.ShapeDtypeStruct((R, w), BF) for w in widths],
        compiler_params=_cp(("arbitrary",)),
        name="qk_prep",
    )(proj, proj, proj, proj, proj, gq, gkv, wuq, wk, wv, gaq, gak, *tabs)


GLA_TM = 256


def _gla_kernel(qf_ref, kf_ref, vf_ref, zf_ref, qb_ref, kb_ref, vb_ref, zb_ref, wg_ref, bg_ref, tri_ref,
                of_ref, ob_ref, st_scr):
    @pl.when(pl.program_id(1) == 0)
    def _():
        st_scr[...] = jnp.zeros_like(st_scr)

    dirs = ((qf_ref, kf_ref, vf_ref, zf_ref, of_ref), (qb_ref, kb_ref, vb_ref, zb_ref, ob_ref))
    las, sts = [], []
    for d, (_, _, _, z_ref, _) in enumerate(dirs):
        z = _dot(z_ref[...], wg_ref[d]) + bg_ref[d]
        las.append((jnp.minimum(z, 0.0) - jnp.log(1.0 + jnp.exp(-jnp.abs(z)))) * (1.0 / GLA_TAU))
        sts.append([st_scr[d, h] for h in range(GLA_H)])
    C = GLA_CHUNK
    nc = GLA_TM // C
    for cc in range(nc):
        for d, (q_ref, k_ref, v_ref, _, o_ref) in enumerate(dirs):
            tri = tri_ref[d]
            keep = tri > 0
            r0 = (cc if d == 0 else nc - 1 - cc) * C
            la = las[d][r0:r0 + C, :]
            h1 = la.astype(BF)
            r1 = la - h1.astype(F32)
            h2 = r1.astype(BF)
            h3 = (r1 - h2.astype(F32)).astype(BF)
            b = _dot(tri, h1) + _dot(tri, h2) + _dot(tri, h3)
            tot = jnp.sum(la, axis=0, keepdims=True)
            q = q_ref[r0:r0 + C, :].astype(F32)
            k = k_ref[r0:r0 + C, :].astype(F32)
            qd = (q * (GLA_DK ** -0.5) * jnp.exp(b)).astype(BF)
            ki = (k * jnp.exp(-b)).astype(BF)
            ke = (k * jnp.exp(tot - b)).astype(BF)
            dec = jnp.exp(tot)
            v = v_ref[r0:r0 + C, :]
            for h in range(GLA_H):
                sl = slice(GLA_DK * h, GLA_DK * (h + 1))
                sv = slice(GLA_DV * h, GLA_DV * (h + 1))
                att = jnp.where(keep, _dot_nt(qd[:, sl], ki[:, sl]), 0.0).astype(BF)
                st = sts[d][h]
                o_ref[r0:r0 + C, sv] = _dot(att, v[:, sv]) + _dot_nt(qd[:, sl], st.astype(BF))
                sts[d][h] = st * dec[:, sl] + _dot_tn(v[:, sv], ke[:, sl])
    for d in range(2):
        for h in range(GLA_H):
            st_scr[d, h] = sts[d][h]


def _gla(proj, wg, bg, tri, *, B, S, RL):
    R = proj.shape[0]
    tm = GLA_TM
    nj = S // tm + 1
    lat0 = lambda b: b * (S // tm)

    def rb(d):
        def f(b, j):
            jl = (j - 1) if d == 0 else (S // tm - j)
            return jnp.where(j == 0, RL // tm + b, lat0(b) + jl)
        return f

    col = lambda d, w, off: pl.BlockSpec((tm, w), lambda b, j: (rb(d)(b, j), off // w))
    full = lambda a: pl.BlockSpec(a.shape, lambda b, j: (0,) * a.ndim)
    ins = lambda d: [col(d, 256, OFF_GQ), col(d, 256, OFF_GK), col(d, 512, OFF_GV), col(d, 128, OFF_KRZ)]
    out = lambda d: pl.BlockSpec((tm, GLA_H * GLA_DV), lambda b, j: (rb(d)(b, j), 0))
    return pl.pallas_call(
        _gla_kernel,
        grid=(B, nj),
        in_specs=ins(0) + ins(1) + [full(wg), full(bg), full(tri)],
        out_specs=[out(0), out(1)],
        out_shape=[jax.ShapeDtypeStruct((R, GLA_H * GLA_DV), F32)] * 2,
        scratch_shapes=[pltpu.VMEM((2, GLA_H, GLA_DV, GLA_DK), F32)],
        compiler_params=_cp(("arbitrary", "arbitrary")),
        name="gla",
    )(*([proj] * 8), wg, bg, tri)


def _flash_kernel(*refs, nseg, seg_lens, tq, dv, ck):
    q_ref = refs[0]
    k_refs = refs[1:1 + nseg]
    v_refs = refs[1 + nseg:1 + 2 * nseg]
    o_ref = refs[1 + 2 * nseg]
    s_scr = refs[2 + 2 * nseg]
    q = q_ref[...]
    m = jnp.full((tq, 128), -jnp.inf, F32)
    chunks = []
    off = 0
    for si in range(nseg):
        for c0 in range(0, seg_lens[si], ck):
            cl = min(ck, seg_lens[si] - c0)
            s = _dot_nt(q, k_refs[si][c0:c0 + cl, :])
            s_scr[:, off:off + cl] = s
            for l0 in range(0, cl, 128):
                m = jnp.maximum(m, s[:, l0:l0 + 128])
            chunks.append((si, c0, cl, off))
            off += cl
    mrow = jnp.max(m, axis=1, keepdims=True)
    l = jnp.zeros((tq, 128), F32)
    acc = jnp.zeros((tq, dv), F32)
    for si, c0, cl, off in chunks:
        p = jnp.exp2(s_scr[:, off:off + cl] - mrow)
        for l0 in range(0, cl, 128):
            l = l + p[:, l0:l0 + 128]
        acc = acc + _dot(p.astype(BF), v_refs[si][c0:c0 + cl, :])
    o_ref[...] = (acc / jnp.sum(l, axis=1, keepdims=True)).astype(o_ref.dtype)


def _attention(qa, ka, va, *, B, S, RL, hq, group, dk, dv, vbase, ctx_queries, name):
    cb = RL // CT
    kh = lambda h: h // group
    if ctx_queries:
        tq, nq, n_out = CT, 1, B * CT
        q_spec = pl.BlockSpec((tq, dk), lambda b, h, i: (cb + b, h))
        o_spec = pl.BlockSpec((tq, dv), lambda b, h, i: (b, h))
        k_specs = [pl.BlockSpec((CT, dk), lambda b, h, i: (cb + b, kh(h)))]
        v_specs = [pl.BlockSpec((CT, dv), lambda b, h, i: (cb + b, vbase + kh(h)))]
        seg_lens = (CT,)
    else:
        tq, nq, n_out = 256, S // 256, RL
        q_spec = pl.BlockSpec((tq, dk), lambda b, h, i: (b * nq + i, h))
        o_spec = pl.BlockSpec((tq, dv), lambda b, h, i: (b * nq + i, h))
        k_specs = [pl.BlockSpec((CT, dk), lambda b, h, i: (cb + b, kh(h))),
                   pl.BlockSpec((S, dk), lambda b, h, i: (b, kh(h)))]
        v_specs = [pl.BlockSpec((CT, dv), lambda b, h, i: (cb + b, vbase + kh(h))),
                   pl.BlockSpec((S, dv), lambda b, h, i: (b, vbase + kh(h)))]
        seg_lens = (CT, S)
    nseg = len(seg_lens)
    kern = functools.partial(_flash_kernel, nseg=nseg, seg_lens=seg_lens, tq=tq, dv=dv, ck=512)
    return pl.pallas_call(
        kern,
        grid=(B, hq, nq),
        in_specs=[q_spec] + k_specs + v_specs,
        out_specs=o_spec,
        out_shape=jax.ShapeDtypeStruct((n_out, hq * dv), BF),
        scratch_shapes=[pltpu.VMEM((tq, sum(seg_lens)), F32)],
        compiler_params=_cp(("arbitrary", "arbitrary", "arbitrary")),
        name=name,
    )(qa, *([ka] * nseg), *([va] * nseg))


WOUT_TM = 512
COUNT_BITS = 20


def _route(logit, rb):
    T = logit.shape[1]
    gsz = N_EXP // N_GRP
    scores = _sigmoid(logit)
    sel = scores + rb
    ninf = jnp.float32(-jnp.inf)
    sel3 = sel.reshape(N_GRP, gsz, T)
    idx3 = lax.broadcasted_iota(I32, (N_GRP, gsz, T), 1)
    m1 = jnp.max(sel3, axis=1, keepdims=True)
    first = jnp.min(jnp.where(sel3 == m1, idx3, gsz), axis=1, keepdims=True)
    m2 = jnp.max(jnp.where(idx3 == first, ninf, sel3), axis=1, keepdims=True)
    gs = (m1 + m2).reshape(N_GRP, T)
    gi = lax.broadcasted_iota(I32, (N_GRP, T), 0)
    grank = jnp.zeros((N_GRP, T), I32)
    for j in range(N_GRP):
        row = gs[j:j + 1, :]
        grank = grank + ((row > gs) | ((row == gs) & (j < gi))).astype(I32)
    gkeep = (grank < TOPK_GRP).reshape(N_GRP, 1, T)
    selm = jnp.where(jnp.broadcast_to(gkeep, (N_GRP, gsz, T)), sel3, ninf).reshape(N_EXP, T)
    ei = lax.broadcasted_iota(I32, (N_EXP, T), 0)
    erank = jnp.zeros((N_EXP, T), I32)
    for j in range(N_EXP):
        row = selm[j:j + 1, :]
        erank = erank + ((row > selm) | ((row == selm) & (j < ei))).astype(I32)
    chosen = erank < TOP_K
    w = jnp.where(chosen, scores, 0.0)
    wd = w / jnp.sum(w, axis=0, keepdims=True) * ROUTED_SCALE
    return wd, chosen


def _count_to_int(x, nbits):
    out = jnp.zeros(x.shape, I32)
    for b in reversed(range(nbits)):
        ge = x >= float(1 << b)
        out = out + jnp.where(ge, 1 << b, 0)
        x = x - jnp.where(ge, float(1 << b), 0.0)
    return out


def _store_tile_rows(ref, packed, tm):
    for cblk in range(HALF // 128):
        ref[pl.ds(cblk, tm, stride=8), :] = packed[:, 128 * cblk:128 * (cblk + 1)]


def _load_tile_rows(ref, tm):
    los, his = [], []
    for cblk in range(HALF // 128):
        lo, hi = _unpack_halves(ref[pl.ds(cblk, tm, stride=8), :])
        los.append(lo.astype(BF))
        his.append(hi.astype(BF))
    return jnp.concatenate(los, axis=1), jnp.concatenate(his, axis=1)


def _wout_kernel(of_ref, ob_ref, gr_ref, mla_ref, gqa_ref, x_ref, w_ref, gout_ref, gpm_ref, gtm_ref,
                 gpf_ref, shf_ref, scf_ref, rwt_ref, rb_ref, ust_ref, lst_ref,
                 xn_ref, hfp_ref, r8_ref, e8_ref, w8_ref, cnt_ref, cnt_scr):
    @pl.when(pl.program_id(0) == 0)
    def _():
        cnt_scr[...] = jnp.zeros_like(cnt_scr)

    og = of_ref[...] + ob_ref[...]
    gr = gr_ref[...].astype(F32)
    parts = []
    for h in range(GLA_H):
        sv = slice(GLA_DV * h, GLA_DV * (h + 1))
        g = gr[:, sv]
        parts.append((_rms(og[:, sv], gout_ref[...]) * (g * _sigmoid(g))).astype(BF))
    gla = jnp.concatenate(parts, axis=1)
    n_g, n_m = GLA_H * GLA_DV, MLA_H * MLA_DV
    y = (_dot(gla, w_ref[0:n_g, :]) + _dot(mla_ref[...], w_ref[n_g:n_g + n_m, :])
         + _dot(gqa_ref[...], w_ref[n_g + n_m:, :]))
    xn = x_ref[...] + gtm_ref[0] * _rms(y, gpm_ref[...])
    xn_ref[...] = xn
    hf = _rms(xn, gpf_ref[...]) * (1.0 + scf_ref[0]) + shf_ref[0]
    _store_tile_rows(hfp_ref, _pack_halves(hf), WOUT_TM)
    hh, hl = _split2(hf)
    rh, rl = _split2(rwt_ref[...])
    logit = _dot_nt(rh, hh) + _dot_nt(rl, hh) + _dot_nt(rh, hl)
    wd, chosen = _route(logit, rb_ref[...])

    cb = jnp.where(chosen, 1.0, 0.0)
    before = _dot(cb.astype(BF), ust_ref[...])
    rank = cnt_scr[:, 0:1] + before
    slot = _dot(lst_ref[...], cb.astype(BF))
    eid = lax.broadcasted_iota(I32, chosen.shape, 0).astype(F32)
    r8, e8, w8 = [], [], []
    for k in range(TOP_K):
        pick = chosen & (slot == float(k))
        r8.append(jnp.sum(jnp.where(pick, rank, 0.0), axis=0, keepdims=True))
        e8.append(jnp.sum(jnp.where(pick, eid, 0.0), axis=0, keepdims=True))
        w8.append(jnp.sum(jnp.where(pick, wd, 0.0), axis=0, keepdims=True))
    r8_ref[...] = _count_to_int(jnp.concatenate(r8, axis=0), COUNT_BITS)
    e8_ref[...] = _count_to_int(jnp.concatenate(e8, axis=0), 6)
    w8_ref[...] = jnp.concatenate(w8, axis=0)
    cnt_scr[...] = cnt_scr[...] + jnp.sum(cb, axis=1, keepdims=True)
    cnt_ref[...] = _count_to_int(cnt_scr[...], COUNT_BITS)


def _wout(gla_o, proj, mla_o, gqa_o, xr, w, gout, gpm, gtm, gpf, shf, scf, rwt, rb, *, n_rows, tiles_per_batch, nb):
    tm = WOUT_TM
    mod_idx = lambda i: (jnp.minimum(i // tiles_per_batch, nb), 0, 0)
    row = lambda w_: pl.BlockSpec((tm, w_), lambda i: (i, 0))
    full = lambda a: pl.BlockSpec(a.shape, lambda i: (0,) * a.ndim)
    mspec = pl.BlockSpec((1, 1, D), mod_idx)
    it = np.arange(tm)
    ust = jnp.asarray(it[:, None] < it[None, :], BF)
    ie = np.arange(N_EXP)
    lst = jnp.asarray(ie[None, :] < ie[:, None], BF)
    k8 = pl.BlockSpec((TOP_K, tm), lambda i: (0, i))
    return pl.pallas_call(
        _wout_kernel,
        grid=(n_rows // tm,),
        in_specs=[row(512), row(512),
                  pl.BlockSpec((tm, 512), lambda i: (i, OFF_GR // 512)),
                  row(768), row(768), row(D), full(w), full(gout), full(gpm), mspec,
                  full(gpf), mspec, mspec, full(rwt), full(rb), full(ust), full(lst)],
        out_specs=[row(D), pl.BlockSpec((tm * 8, 128), lambda i: (i, 0)), k8, k8, k8,
                   pl.BlockSpec((N_EXP, 128), lambda i: (0, 0))],
        out_shape=[jax.ShapeDtypeStruct((n_rows, D), F32), jax.ShapeDtypeStruct((n_rows * 8, 128), PACKED),
                   jax.ShapeDtypeStruct((TOP_K, n_rows), I32), jax.ShapeDtypeStruct((TOP_K, n_rows), I32),
                   jax.ShapeDtypeStruct((TOP_K, n_rows), F32), jax.ShapeDtypeStruct((N_EXP, 128), I32)],
        scratch_shapes=[pltpu.VMEM((N_EXP, 128), F32)],
        compiler_params=_cp(("arbitrary",)),
        name="wout_route",
    )(gla_o[0], gla_o[1], proj, mla_o, gqa_o, xr, w, gout, gpm, gtm, gpf, shf, scf, rwt, rb, ust, lst)


MOE_TM = 512
DISP_TM = 256
COMB_TM = 128


def _tile_row(ref, r):
    return ref.at[pl.ds(pl.multiple_of(r * 8, 8), 8), :]


def _dispatch_kernel(cnt_ref, pst_ref, ntl_ref, nu_ref, pos_ref, hfp_ref, xs_ref, zero_scr, sem, zsem):
    i = pl.program_id(0)
    tm = DISP_TM
    tile_rows = MOE_TM * 8
    n_tiles = xs_ref.shape[0] // tile_rows

    def issue(t, carry):
        for k in range(TOP_K):
            p = pos_ref[0, 0, k * tm + t]
            pltpu.make_async_copy(_tile_row(hfp_ref, t), _tile_row(xs_ref, p), sem).start(priority=k % 2)
        return carry

    lax.fori_loop(0, tm, issue, 0, unroll=2)

    @pl.when(i == 0)
    def _():
        zero_scr[...] = _packed_zeros(zero_scr.shape)

        def per_expert(e, carry):
            cnt = cnt_ref[e]
            padlen = ntl_ref[e] * MOE_TM - cnt
            off = pst_ref[e] + cnt

            pieces = []
            bit = MOE_TM // 2
            while bit >= 1:
                take = padlen & bit
                dst = xs_ref.at[pl.ds(pl.multiple_of(off * 8, 8), bit * 8), :]
                pieces.append((take != 0, pltpu.make_async_copy(zero_scr.at[pl.ds(0, bit * 8), :], dst, zsem)))
                off = off + take
                bit //= 2
            for go, cp in pieces:
                pl.when(go)(cp.start)
            for go, cp in pieces:
                pl.when(go)(cp.wait)
            return carry

        lax.fori_loop(0, N_EXP, per_expert, 0)

        def ztile(t):
            dst = xs_ref.at[pl.ds(pl.multiple_of(t * tile_rows, tile_rows), tile_rows), :]
            return pltpu.make_async_copy(zero_scr, dst, zsem)

        lax.fori_loop(nu_ref[0], n_tiles, lambda t, cr: (ztile(t).start(), cr)[1], 0)
        lax.fori_loop(nu_ref[0], n_tiles, lambda t, cr: (ztile(t).wait(), cr)[1], 0)

    for k in range(TOP_K):
        pltpu.make_async_copy(hfp_ref, xs_ref.at[pl.ds(0, tm * 8), :], sem).wait()


def _dispatch(cnt, pst, ntl, n_used, pos_t, hfp, *, n_rows_sorted):
    n = hfp.shape[0] // 8
    tm = DISP_TM
    grid_spec = pltpu.PrefetchScalarGridSpec(
        num_scalar_prefetch=4,
        grid=(n // tm,),
        in_specs=[pl.BlockSpec((1, 1, tm * TOP_K), lambda i, *_: (i, 0, 0), memory_space=pltpu.SMEM),
                  pl.BlockSpec((tm * 8, 128), lambda i, *_: (i, 0))],
        out_specs=pl.BlockSpec(memory_space=pl.ANY),
        scratch_shapes=[pltpu.VMEM((MOE_TM * 8, 128), PACKED), pltpu.SemaphoreType.DMA, pltpu.SemaphoreType.DMA],
    )
    return pl.pallas_call(
        _dispatch_kernel,
        grid_spec=grid_spec,
        out_shape=jax.ShapeDtypeStruct((n_rows_sorted * 8, 128), PACKED),
        compiler_params=_cp(("arbitrary",)),
        name="moe_dispatch",
    )(cnt, pst, ntl, n_used, pos_t, hfp)


def _gmm_kernel(te_ref, nu_ref, nxt_ref, ord_ref, x_ref, w1_ref, w3_ref, w2_ref, y_ref,
                w1f, w3f, w2f, w1b, w3b, w2b, wsem, *, layer):
    i = pl.program_id(0)
    used = i < nu_ref[0]

    def weight_copies(e, slot):
        row = layer * N_EXP + e
        return [pltpu.make_async_copy(src.at[row], dst.at[slot], wsem.at[slot])
                for src, dst in ((w1_ref, w1f), (w3_ref, w3f), (w2_ref, w2f))]

    @pl.when(used)
    def _():
        e = te_ref[i]
        changed = jnp.logical_or(i == 0, te_ref[jnp.maximum(i - 1, 0)] != e)

        @pl.when(changed)
        def _():
            slot = ord_ref[e] % 2

            @pl.when(i == 0)
            def _():
                for cp in weight_copies(e, slot):
                    cp.start()

            for cp in weight_copies(e, slot):
                cp.wait()
            nxt = nxt_ref[e]

            @pl.when(nxt >= 0)
            def _():
                for cp in weight_copies(nxt, 1 - slot):
                    cp.start()

            w1b[...] = w1f[slot].astype(BF)
            w3b[...] = w3f[slot].astype(BF)
            w2b[...] = w2f[slot].astype(BF)

        xa, xb = _load_tile_rows(x_ref, MOE_TM)
        h1 = _dot(xa, w1b[0:HALF, :]) + _dot(xb, w1b[HALF:, :])
        h3 = _dot(xa, w3b[0:HALF, :]) + _dot(xb, w3b[HALF:, :])
        h = (h1 * _sigmoid(h1) * h3).astype(BF)
        _store_tile_rows(y_ref, _pack_halves(_dot(h, w2b[...])), MOE_TM)

    @pl.when(jnp.logical_not(used))
    def _():
        y_ref[...] = _packed_zeros(y_ref.shape)


def _gmm(tile_e, n_used, ntl, xs, w1, w3, w2, layer):
    tm = MOE_TM
    n_tiles = xs.shape[0] // (tm * 8)
    ie = jnp.arange(N_EXP, dtype=I32)
    has = ntl > 0
    later = (ie[None, :] > ie[:, None]) & has[None, :]
    nxt = jnp.min(jnp.where(later, ie[None, :], N_EXP), axis=1)
    nxt = jnp.where(nxt == N_EXP, -1, nxt).astype(I32)
    ordn = (jnp.sum(((ie[None, :] <= ie[:, None]) & has[None, :]).astype(I32), axis=1) - 1).astype(I32)
    any_spec = pl.BlockSpec(memory_space=pl.ANY)
    grid_spec = pltpu.PrefetchScalarGridSpec(
        num_scalar_prefetch=4,
        grid=(n_tiles,),
        in_specs=[pl.BlockSpec((tm * 8, 128), lambda i, te, nu, *_: (jnp.minimum(i, nu[0] - 1), 0)),
                  any_spec, any_spec, any_spec],
        out_specs=pl.BlockSpec((tm * 8, 128), lambda i, *_: (i, 0)),
        scratch_shapes=[pltpu.VMEM((2, D, D_EXP), F32), pltpu.VMEM((2, D, D_EXP), F32), pltpu.VMEM((2, D_EXP, D), F32),
                        pltpu.VMEM((D, D_EXP), BF), pltpu.VMEM((D, D_EXP), BF), pltpu.VMEM((D_EXP, D), BF),
                        pltpu.SemaphoreType.DMA((2,))],
    )
    return pl.pallas_call(
        functools.partial(_gmm_kernel, layer=layer),
        grid_spec=grid_spec,
        out_shape=jax.ShapeDtypeStruct(xs.shape, PACKED),
        compiler_params=_cp(("arbitrary",)),
        name="moe_experts",
    )(tile_e, n_used, nxt, ordn, xs, w1, w3, w2)


def _combine_kernel(pos_ref, posn_ref, w8_ref, hfp_ref, x_ref, s1_ref, s3_ref, s2_ref, gpost_ref, gtf_ref, ys_ref,
                    o_ref, ybuf_a, ybuf_b, sem_a, sem_b):
    tm = COMB_TM
    i = pl.program_id(0)
    last = pl.num_programs(0) - 1

    def gather(pref, buf, sem):
        for t in range(tm):
            for k in range(TOP_K):
                p = pref[0, 0, k * tm + t]
                pltpu.make_async_copy(_tile_row(ys_ref, p), buf.at[k, pl.ds(t * 8, 8), :], sem).start(priority=k % 2)

    def drain(buf, sem):
        for k in range(TOP_K):
            pltpu.make_async_copy(ys_ref.at[pl.ds(0, tm * 8), :], buf.at[k], sem).wait()

    @pl.when(i == 0)
    def _():
        gather(pos_ref, ybuf_a, sem_a)

    def body(cur, sem_c, nxt, sem_n):
        gather(posn_ref, nxt, sem_n)
        xa, xb = _load_tile_rows(hfp_ref, tm)
        h1 = _dot(xa, s1_ref[0:HALF, :]) + _dot(xb, s1_ref[HALF:, :])
        h3 = _dot(xa, s3_ref[0:HALF, :]) + _dot(xb, s3_ref[HALF:, :])
        sh = _dot((h1 * _sigmoid(h1) * h3).astype(BF), s2_ref[...])
        drain(cur, sem_c)
        w8 = w8_ref[...]
        wk = [w8[:, k:k + 1] for k in range(TOP_K)]
        los, his = [], []
        for cblk in range(HALF // 128):
            a_lo = sh[:, 128 * cblk:128 * (cblk + 1)]
            a_hi = sh[:, HALF + 128 * cblk:HALF + 128 * (cblk + 1)]
            for k in range(TOP_K):
                lo, hi = _unpack_halves(cur[k, pl.ds(cblk, tm, stride=8), :])
                a_lo = a_lo + wk[k] * lo
                a_hi = a_hi + wk[k] * hi
            los.append(a_lo)
            his.append(a_hi)
        yf = jnp.concatenate(los + his, axis=1)
        o_ref[...] = x_ref[...] + gtf_ref[0] * _rms(yf, gpost_ref[...])

        @pl.when(i == last)
        def _():
            drain(nxt, sem_n)

    @pl.when(i % 2 == 0)
    def _():
        body(ybuf_a, sem_a, ybuf_b, sem_b)

    @pl.when(i % 2 == 1)
    def _():
        body(ybuf_b, sem_b, ybuf_a, sem_a)


def _combine(pos_t, w8, hfp, xn, s1, s3, s2, gpost, gtf, ys, *, tiles_per_batch, nb):
    n = hfp.shape[0] // 8
    tm = COMB_TM
    nt = n // tm
    mod_idx = lambda i: (jnp.minimum(i // tiles_per_batch, nb), 0, 0)
    full = lambda a: pl.BlockSpec(a.shape, lambda i: (0,) * a.ndim)
    buf = pltpu.VMEM((TOP_K, tm * 8, 128), PACKED)
    return pl.pallas_call(
        _combine_kernel,
        grid=(nt,),
        in_specs=[pl.BlockSpec((1, 1, tm * TOP_K), lambda i: (i, 0, 0), memory_space=pltpu.SMEM),
                  pl.BlockSpec((1, 1, tm * TOP_K), lambda i: (jnp.minimum(i + 1, nt - 1), 0, 0),
                               memory_space=pltpu.SMEM),
                  pl.BlockSpec((tm, TOP_K), lambda i: (i, 0)),
                  pl.BlockSpec((tm * 8, 128), lambda i: (i, 0)),
                  pl.BlockSpec((tm, D), lambda i: (i, 0)),
                  full(s1), full(s3), full(s2), full(gpost),
                  pl.BlockSpec((1, 1, D), mod_idx),
                  pl.BlockSpec(memory_space=pl.ANY)],
        out_specs=pl.BlockSpec((tm, D), lambda i: (i, 0)),
        out_shape=jax.ShapeDtypeStruct((n, D), F32),
        scratch_shapes=[buf, buf, pltpu.SemaphoreType.DMA, pltpu.SemaphoreType.DMA],
        compiler_params=_cp(("arbitrary",)),
        name="moe_combine",
    )(pos_t, pos_t, w8, hfp, xn, s1, s3, s2, gpost, gtf, ys)


def _dispatch_tables(counts, r8, e8):
    n = r8.shape[1]
    ntl = (counts + MOE_TM - 1) // MOE_TM
    tend = jnp.cumsum(ntl)
    pst = (tend - ntl) * MOE_TM
    onehot = e8[None] == jnp.arange(N_EXP, dtype=I32)[:, None, None]
    pos = r8 + jnp.sum(jnp.where(onehot, pst[:, None, None], 0), axis=0)
    n_tiles = -(-(n * TOP_K) // MOE_TM) + N_EXP
    tile_e = jnp.sum((jnp.arange(n_tiles, dtype=I32)[:, None] >= tend[None, :]).astype(I32), axis=1)
    tile_e = jnp.minimum(tile_e, N_EXP - 1)
    return pst.astype(I32), ntl.astype(I32), pos.astype(I32), tile_e.astype(I32), tend[-1:].astype(I32), n_tiles


def _pos_blocks(pos, tm):
    k, n = pos.shape
    return pos.reshape(k, n // tm, tm).transpose(1, 0, 2).reshape(n // tm, 1, k * tm)


def _inproj_weight(w_in):
    offs = np.cumsum((0, 256, 256, 512, 512, 16, 16, 384, 512, 64, 768, 256, 256))
    p = [w_in[:, offs[i]:offs[i + 1]] for i in range(12)]
    gq, gk, gv, gr, gzf, gzb, cq, ckv, kr, aq, ak, av = p
    krz = jnp.concatenate([kr, gzf, gzb, jnp.zeros((D, 32), w_in.dtype)], axis=1)
    return jnp.concatenate([gv, gr, ckv, aq, cq, krz, gq, gk, ak, av], axis=1).astype(BF)


def _mla_weights(w_uq, w_ukv):
    wq = w_uq.reshape(MLA_QR, MLA_H, MLA_NOPE + MLA_ROPE)
    wq = jnp.concatenate([wq, jnp.zeros((MLA_QR, MLA_H, 256 - MLA_NOPE - MLA_ROPE), wq.dtype)], axis=2)
    wkv = w_ukv.reshape(MLA_KVR, MLA_H, MLA_NOPE + MLA_DV)
    wk = wkv[:, :, :MLA_NOPE].reshape(MLA_KVR, MLA_H * MLA_NOPE)
    wv = wkv[:, :, MLA_NOPE:].reshape(MLA_KVR, MLA_H * MLA_DV)
    return wq.reshape(MLA_QR, MLA_H * 256).astype(BF), wk.astype(BF), wv.astype(BF)


def _gla_gate_weights(wf, bf, wb, bb):
    z = jnp.zeros((128, GLA_H * GLA_DK), F32)
    wg = jnp.stack([z.at[64:80].set(wf), z.at[80:96].set(wb)]).astype(BF)
    bg = jnp.stack([bf, bb]).reshape(2, 1, GLA_H * GLA_DK)
    return wg, bg


def kernel(x, c, ctx, c_ctx, w_mod, b_mod, g_pre_mix, g_post_mix, g_pre_ffn, g_post_ffn, w_in, gla_wg2_f, gla_bg_f, gla_wg2_b, gla_bg_b, gla_g_out, mla_g_q, mla_w_uq, mla_g_kv, mla_w_ukv, gqa_g_q, gqa_g_k, w_out, router_w, router_b, exp_w1, exp_w3, exp_w2, sh_w1, sh_w3, sh_w2):
    B, S, _ = x.shape
    depth = w_mod.shape[0]
    assert ctx.shape[1] == CT and S % 256 == 0 and B < 8 and B * (S + CT) < (1 << COUNT_BITS)
    RL, RC = B * S, B * CT
    R = RL + RC

    xr = jnp.concatenate([x.reshape(RL, D), ctx.reshape(RC, D)], axis=0)
    cvecs = jnp.concatenate([c, c_ctx[None], jnp.zeros((8 - B - 1, D), F32)], axis=0)
    mods = _modulation(cvecs, w_mod, b_mod)
    tabs = _rope_tables(S)
    ii = np.arange(GLA_CHUNK)
    tri = jnp.asarray(np.stack([ii[:, None] >= ii[None, :], ii[:, None] <= ii[None, :]]), BF)

    in_tm = 1024 if (S % 1024 == 0 and RC % 1024 == 0) else 256
    row2 = lambda v: v.reshape(1, -1)
    ew1 = exp_w1.reshape(depth * N_EXP, D, D_EXP)
    ew3 = exp_w3.reshape(depth * N_EXP, D, D_EXP)
    ew2 = exp_w2.reshape(depth * N_EXP, D_EXP, D)

    for l in range(depth):
        want_ctx = l < depth - 1
        m = mods[l].reshape(8, 6, 1, D)
        sh_m, sc_m, gt_m, sh_f, sc_f, gt_f = [m[:, i] for i in range(6)]

        proj = _inproj(xr, row2(g_pre_mix[l]), sh_m, sc_m, _inproj_weight(w_in[l]),
                       tm=in_tm, tiles_per_batch=S // in_tm, nb=B)
        wuq, wk, wv = _mla_weights(mla_w_uq[l], mla_w_ukv[l])
        qm, km, vm, qg, kg = _prep(proj, row2(mla_g_q[l]), row2(mla_g_kv[l]), wuq, wk, wv,
                                   row2(gqa_g_q[l]), row2(gqa_g_k[l]), tabs, S=S, RL=RL)
        wg, bg = _gla_gate_weights(gla_wg2_f[l], gla_bg_f[l], gla_wg2_b[l], gla_bg_b[l])
        gla_o = _gla(proj, wg, bg, tri, B=B, S=S, RL=RL)

        att = functools.partial(_attention, B=B, S=S, RL=RL)
        mla = functools.partial(att, qm, km, vm, hq=MLA_H, group=1, dk=256, dv=MLA_DV, vbase=0)
        gqa = functools.partial(att, qg, kg, proj, hq=GQA_H, group=GQA_H // GQA_KV, dk=GQA_DH, dv=GQA_DH,
                                vbase=OFF_AV // GQA_DH)
        mla_o = mla(ctx_queries=False, name="mla_lat")
        gqa_o = gqa(ctx_queries=False, name="gqa_lat")
        if want_ctx:
            mla_o = jnp.concatenate([mla_o, mla(ctx_queries=True, name="mla_ctx")], axis=0)
            gqa_o = jnp.concatenate([gqa_o, gqa(ctx_queries=True, name="gqa_ctx")], axis=0)
        n_rows = R if want_ctx else RL

        xn, hfp, r8, e8, w8, cnt = _wout(gla_o, proj, mla_o, gqa_o, xr, w_out[l].astype(BF), row2(gla_g_out[l]),
                                         row2(g_post_mix[l]), gt_m, row2(g_pre_ffn[l]), sh_f, sc_f,
                                         router_w[l].T, router_b[l].reshape(N_EXP, 1),
                                         n_rows=n_rows, tiles_per_batch=S // WOUT_TM, nb=B)

        counts = cnt[:, 0]
        pst, ntl, pos, tile_e, n_used, n_tiles = _dispatch_tables(counts, r8, e8)
        xs = _dispatch(counts, pst, ntl, n_used, _pos_blocks(pos, DISP_TM), hfp, n_rows_sorted=n_tiles * MOE_TM)
        ys = _gmm(tile_e, n_used, ntl, xs, ew1, ew3, ew2, l)
        xo = _combine(_pos_blocks(pos, COMB_TM), w8.T, hfp, xn,
                      sh_w1[l].astype(BF), sh_w3[l].astype(BF), sh_w2[l].astype(BF),
                      row2(g_post_ffn[l]), gt_f, ys, tiles_per_batch=S // COMB_TM, nb=B)
        xr = xo
    return xr[:RL].reshape(B, S, D)
```

```python
import functools
import math

import numpy as np
import jax
import jax.numpy as jnp
from jax import lax
from jax.experimental import pallas as pl
from jax.experimental.pallas import tpu as pltpu

F32 = jnp.float32
BF = jnp.bfloat16
PACKED = jnp.uint32
I32 = jnp.int32

D = 2048
GRID_W = 64
CT = 256
ROPE_THETA = 10000.0
EPS = 1e-6
GLA_H, GLA_DK, GLA_DV, GLA_RANK, GLA_TAU, GLA_CHUNK = 4, 64, 128, 16, 16.0, 64
MLA_H, MLA_QR, MLA_KVR, MLA_NOPE, MLA_ROPE, MLA_DV = 6, 384, 512, 128, 64, 128
GQA_H, GQA_KV, GQA_DH = 6, 2, 128
N_EXP, TOP_K, N_GRP, TOPK_GRP, D_EXP, D_SH = 64, 8, 8, 4, 512, 512
ROUTED_SCALE = 2.5
LOG2E = math.log2(math.e)
HALF = D // 2

OFF_GV, OFF_GR, OFF_CKV, OFF_AQ, OFF_CQ, OFF_KRZ, OFF_GQ, OFF_GK, OFF_AK, OFF_AV = (
    0, 512, 1024, 1536, 2304, 2688, 2816, 3072, 3328, 3584)
PROJ_W = 3840
VMEM_LIMIT = 56 * 1024 * 1024


def _cp(sem, vmem=None):
    return pltpu.CompilerParams(dimension_semantics=sem, vmem_limit_bytes=vmem or VMEM_LIMIT)


def _dot(a, b):
    return jnp.dot(a, b, preferred_element_type=F32)


def _dot_nt(a, b):
    return lax.dot_general(a, b, (((1,), (1,)), ((), ())), preferred_element_type=F32)


def _dot_tn(a, b):
    return lax.dot_general(a, b, (((0,), (0,)), ((), ())), preferred_element_type=F32)


def _sigmoid(x):
    return 1.0 / (1.0 + jnp.exp(-x))


def _rms(x, g):
    return x * lax.rsqrt(jnp.mean(x * x, axis=-1, keepdims=True) + EPS) * g


def _split2(x):
    hi = x.astype(BF)
    return hi, (x - hi.astype(F32)).astype(BF)


def _pack_halves(y):
    return pltpu.pack_elementwise([y[:, :HALF], y[:, HALF:]], packed_dtype=BF)


def _packed_zeros(shape):
    z = jnp.zeros(shape, F32)
    return pltpu.pack_elementwise([z, z], packed_dtype=BF)


def _unpack_halves(w):
    lo = pltpu.unpack_elementwise(w, index=0, packed_dtype=BF, unpacked_dtype=F32)
    hi = pltpu.unpack_elementwise(w, index=1, packed_dtype=BF, unpacked_dtype=F32)
    return lo, hi


def _mod_kernel(a_ref, w_ref, b_ref, o_ref):
    a = a_ref[...]
    a = a * _sigmoid(a)
    ah, al = _split2(a)
    wh, wl = _split2(w_ref[0])
    o_ref[0] = _dot(ah, wh) + _dot(al, wh) + _dot(ah, wl) + b_ref[0]


def _modulation(cvecs, w_mod, b_mod):
    L, _, n = w_mod.shape
    tn = 1024
    return pl.pallas_call(
        _mod_kernel,
        grid=(L, n // tn),
        in_specs=[pl.BlockSpec((8, D), lambda l, j: (0, 0)),
                  pl.BlockSpec((1, D, tn), lambda l, j: (l, 0, j)),
                  pl.BlockSpec((1, 1, tn), lambda l, j: (l, 0, j))],
        out_specs=pl.BlockSpec((1, 8, tn), lambda l, j: (l, 0, j)),
        out_shape=jax.ShapeDtypeStruct((L, 8, n), F32),
        compiler_params=_cp(("arbitrary", "arbitrary")),
        name="modulation",
    )(cvecs, w_mod, b_mod.reshape(L, 1, n))


def _inproj_kernel(x_ref, g_ref, sh_ref, sc_ref, w_ref, o_ref, a_scr):
    @pl.when(pl.program_id(1) == 0)
    def _():
        y = _rms(x_ref[...], g_ref[...])
        a_scr[...] = (y * (1.0 + sc_ref[0]) + sh_ref[0]).astype(BF)

    o_ref[...] = _dot(a_scr[...], w_ref[...]).astype(o_ref.dtype)


def _inproj(xr, g, sh, sc, w, *, tm, tiles_per_batch, nb):
    R = xr.shape[0]
    tn = 768
    mod_idx = lambda i, j: (jnp.minimum(i // tiles_per_batch, nb), 0, 0)
    return pl.pallas_call(
        _inproj_kernel,
        grid=(R // tm, PROJ_W // tn),
        in_specs=[pl.BlockSpec((tm, D), lambda i, j: (i, 0)),
                  pl.BlockSpec((1, D), lambda i, j: (0, 0)),
                  pl.BlockSpec((1, 1, D), mod_idx),
                  pl.BlockSpec((1, 1, D), mod_idx),
                  pl.BlockSpec((D, tn), lambda i, j: (0, j))],
        out_specs=pl.BlockSpec((tm, tn), lambda i, j: (i, j)),
        out_shape=jax.ShapeDtypeStruct((R, PROJ_W), BF),
        scratch_shapes=[pltpu.VMEM((tm, D), BF)],
        compiler_params=_cp(("arbitrary", "arbitrary")),
        name="inproj",
    )(xr, g, sh, sc, w)


PREP_TM = 256
QS_MLA = (MLA_NOPE + MLA_ROPE) ** -0.5 * LOG2E
QS_GQA = GQA_DH ** -0.5 * LOG2E


def _prep_kernel(cq_ref, ckv_ref, krz_ref, aq_ref, ak_ref, gq_ref, gkv_ref, wuq_ref, wk_ref, wv_ref,
                 gaq_ref, gak_ref, cg_ref, sg_ref, cm_ref, sm_ref,
                 qm_ref, km_ref, vm_ref, qg_ref, kg_ref):
    lane = lax.broadcasted_iota(I32, (PREP_TM, 128), 1)

    def rope(t, c, s, half):
        nxt = pltpu.roll(t, 128 - half, 1)
        prv = pltpu.roll(t, half, 1)
        return t * c + jnp.where((lane % (2 * half)) < half, nxt, prv) * s

    cm, sm = cm_ref[...], sm_ref[...]
    cg, sg = cg_ref[...], sg_ref[...]

    q = _dot(_rms(cq_ref[...].astype(F32), gq_ref[...]).astype(BF), wuq_ref[...])
    for h in range(MLA_H):
        a = 256 * h
        qm_ref[:, a:a + 128] = (q[:, a:a + 128] * QS_MLA).astype(BF)
        qm_ref[:, a + 128:a + 256] = (rope(q[:, a + 128:a + 256], cm, sm, MLA_ROPE // 4) * QS_MLA).astype(BF)

    n = _rms(ckv_ref[...].astype(F32), gkv_ref[...]).astype(BF)
    kn = _dot(n, wk_ref[...])
    vm_ref[...] = _dot(n, wv_ref[...]).astype(BF)
    kr = rope(krz_ref[...].astype(F32), cm, sm, MLA_ROPE // 4).astype(BF)
    for h in range(MLA_H):
        a = 256 * h
        km_ref[:, a:a + 128] = kn[:, 128 * h:128 * h + 128].astype(BF)
        km_ref[:, a + 128:a + 256] = kr

    for h in range(GQA_H):
        t = _rms(aq_ref[:, 128 * h:128 * h + 128].astype(F32), gaq_ref[...])
        qg_ref[:, 128 * h:128 * h + 128] = (rope(t, cg, sg, GQA_DH // 4) * QS_GQA).astype(BF)
    for h in range(GQA_KV):
        t = _rms(ak_ref[:, 128 * h:128 * h + 128].astype(F32), gak_ref[...])
        kg_ref[:, 128 * h:128 * h + 128] = rope(t, cg, sg, GQA_DH // 4).astype(BF)


def _rope_tables(S):
    pos = np.arange(S)
    rows, cols = pos // GRID_W, pos % GRID_W

    def tab(d, width):
        half = d // 2
        inv = ROPE_THETA ** (-np.arange(0, half, 2, dtype=np.float64) / half)
        ar, ac = rows[:, None] * inv[None], cols[:, None] * inv[None]
        c = np.concatenate([np.cos(ar), np.cos(ar), np.cos(ac), np.cos(ac)], 1)
        s = np.concatenate([-np.sin(ar), np.sin(ar), -np.sin(ac), np.sin(ac)], 1)
        ci = np.ones((CT, d)); si = np.zeros((CT, d))
        c = np.concatenate([c, ci], 0); s = np.concatenate([s, si], 0)
        pad = np.zeros((S + CT, width - d))
        return (jnp.asarray(np.concatenate([c, pad], 1), F32), jnp.asarray(np.concatenate([s, pad], 1), F32))

    return tab(GQA_DH, 128) + tab(MLA_ROPE, 128)


def _prep(proj, gq, gkv, wuq, wk, wv, gaq, gak, tabs, *, S, RL):
    R = proj.shape[0]
    tm = PREP_TM
    nlat = RL // tm
    spb = S // tm
    tab_idx = lambda i: (jnp.where(i < nlat, i % spb, spb), 0)
    col = lambda w, off: pl.BlockSpec((tm, w), lambda i: (i, off // w))
    full = lambda a: pl.BlockSpec(a.shape, lambda i: (0,) * a.ndim)
    tspec = pl.BlockSpec((tm, 128), tab_idx)
    widths = [MLA_H * 256, MLA_H * 256, MLA_H * MLA_DV, GQA_H * GQA_DH, GQA_KV * GQA_DH]
    return pl.pallas_call(
        _prep_kernel,
        grid=(R // tm,),
        in_specs=[col(MLA_QR, OFF_CQ), col(MLA_KVR, OFF_CKV), col(128, OFF_KRZ), col(768, OFF_AQ), col(256, OFF_AK),
                  full(gq), full(gkv), full(wuq), full(wk), full(wv), full(gaq), full(gak),
                  tspec, tspec, tspec, tspec],
        out_specs=[pl.BlockSpec((tm, w), lambda i: (i, 0)) for w in widths],
        out_shape=[jax.ShapeDtypeStruct((R, w), BF) for w in widths],
        compiler_params=_cp(("arbitrary",)),
        name="qk_prep",
    )(proj, proj, proj, proj, proj, gq, gkv, wuq, wk, wv, gaq, gak, *tabs)


GLA_TM = 256


def _gla_kernel(qf_ref, kf_ref, vf_ref, zf_ref, qb_ref, kb_ref, vb_ref, zb_ref, wg_ref, bg_ref, tri_ref,
                of_ref, ob_ref, st_scr):
    @pl.when(pl.program_id(1) == 0)
    def _():
        st_scr[...] = jnp.zeros_like(st_scr)

    dirs = ((qf_ref, kf_ref, vf_ref, zf_ref, of_ref), (qb_ref, kb_ref, vb_ref, zb_ref, ob_ref))
    las, sts = [], []
    for d, (_, _, _, z_ref, _) in enumerate(dirs):
        z = _dot(z_ref[...], wg_ref[d]) + bg_ref[d]
        las.append((jnp.minimum(z, 0.0) - jnp.log(1.0 + jnp.exp(-jnp.abs(z)))) * (1.0 / GLA_TAU))
        sts.append([st_scr[d, h] for h in range(GLA_H)])
    C = GLA_CHUNK
    nc = GLA_TM // C
    for cc in range(nc):
        for d, (q_ref, k_ref, v_ref, _, o_ref) in enumerate(dirs):
            tri = tri_ref[d]
            keep = tri > 0
            r0 = (cc if d == 0 else nc - 1 - cc) * C
            la = las[d][r0:r0 + C, :]
            h1 = la.astype(BF)
            r1 = la - h1.astype(F32)
            h2 = r1.astype(BF)
            h3 = (r1 - h2.astype(F32)).astype(BF)
            b = _dot(tri, h1) + _dot(tri, h2) + _dot(tri, h3)
            tot = jnp.sum(la, axis=0, keepdims=True)
            q = q_ref[r0:r0 + C, :].astype(F32)
            k = k_ref[r0:r0 + C, :].astype(F32)
            qd = (q * (GLA_DK ** -0.5) * jnp.exp(b)).astype(BF)
            ki = (k * jnp.exp(-b)).astype(BF)
            ke = (k * jnp.exp(tot - b)).astype(BF)
            dec = jnp.exp(tot)
            v = v_ref[r0:r0 + C, :]
            for h in range(GLA_H):
                sl = slice(GLA_DK * h, GLA_DK * (h + 1))
                sv = slice(GLA_DV * h, GLA_DV * (h + 1))
                att = jnp.where(keep, _dot_nt(qd[:, sl], ki[:, sl]), 0.0).astype(BF)
                st = sts[d][h]
                o_ref[r0:r0 + C, sv] = _dot(att, v[:, sv]) + _dot_nt(qd[:, sl], st.astype(BF))
                sts[d][h] = st * dec[:, sl] + _dot_tn(v[:, sv], ke[:, sl])
    for d in range(2):
        for h in range(GLA_H):
            st_scr[d, h] = sts[d][h]


def _gla(proj, wg, bg, tri, *, B, S, RL):
    R = proj.shape[0]
    tm = GLA_TM
    nj = S // tm + 1
    lat0 = lambda b: b * (S // tm)

    def rb(d):
        def f(b, j):
            jl = (j - 1) if d == 0 else (S // tm - j)
            return jnp.where(j == 0, RL // tm + b, lat0(b) + jl)
        return f

    col = lambda d, w, off: pl.BlockSpec((tm, w), lambda b, j: (rb(d)(b, j), off // w))
    full = lambda a: pl.BlockSpec(a.shape, lambda b, j: (0,) * a.ndim)
    ins = lambda d: [col(d, 256, OFF_GQ), col(d, 256, OFF_GK), col(d, 512, OFF_GV), col(d, 128, OFF_KRZ)]
    out = lambda d: pl.BlockSpec((tm, GLA_H * GLA_DV), lambda b, j: (rb(d)(b, j), 0))
    return pl.pallas_call(
        _gla_kernel,
        grid=(B, nj),
        in_specs=ins(0) + ins(1) + [full(wg), full(bg), full(tri)],
        out_specs=[out(0), out(1)],
        out_shape=[jax.ShapeDtypeStruct((R, GLA_H * GLA_DV), F32)] * 2,
        scratch_shapes=[pltpu.VMEM((2, GLA_H, GLA_DV, GLA_DK), F32)],
        compiler_params=_cp(("arbitrary", "arbitrary")),
        name="gla",
    )(*([proj] * 8), wg, bg, tri)


def _flash_kernel(*refs, nseg, seg_lens, tq, dv, ck):
    q_ref = refs[0]
    k_refs = refs[1:1 + nseg]
    v_refs = refs[1 + nseg:1 + 2 * nseg]
    o_ref = refs[1 + 2 * nseg]
    s_scr = refs[2 + 2 * nseg]
    q = q_ref[...]
    m = jnp.full((tq, 128), -jnp.inf, F32)
    chunks = []
    off = 0
    for si in range(nseg):
        for c0 in range(0, seg_lens[si], ck):
            cl = min(ck, seg_lens[si] - c0)
            s = _dot_nt(q, k_refs[si][c0:c0 + cl, :])
            s_scr[:, off:off + cl] = s
            for l0 in range(0, cl, 128):
                m = jnp.maximum(m, s[:, l0:l0 + 128])
            chunks.append((si, c0, cl, off))
            off += cl
    mrow = jnp.max(m, axis=1, keepdims=True)
    l = jnp.zeros((tq, 128), F32)
    acc = jnp.zeros((tq, dv), F32)
    for si, c0, cl, off in chunks:
        p = jnp.exp2(s_scr[:, off:off + cl] - mrow)
        for l0 in range(0, cl, 128):
            l = l + p[:, l0:l0 + 128]
        acc = acc + _dot(p.astype(BF), v_refs[si][c0:c0 + cl, :])
    o_ref[...] = (acc / jnp.sum(l, axis=1, keepdims=True)).astype(o_ref.dtype)


def _attention(qa, ka, va, *, B, S, RL, hq, group, dk, dv, vbase, ctx_queries, name):
    cb = RL // CT
    kh = lambda h: h // group
    if ctx_queries:
        tq, nq, n_out = CT, 1, B * CT
        q_spec = pl.BlockSpec((tq, dk), lambda b, h, i: (cb + b, h))
        o_spec = pl.BlockSpec((tq, dv), lambda b, h, i: (b, h))
        k_specs = [pl.BlockSpec((CT, dk), lambda b, h, i: (cb + b, kh(h)))]
        v_specs = [pl.BlockSpec((CT, dv), lambda b, h, i: (cb + b, vbase + kh(h)))]
        seg_lens = (CT,)
    else:
        tq, nq, n_out = 256, S // 256, RL
        q_spec = pl.BlockSpec((tq, dk), lambda b, h, i: (b * nq + i, h))
        o_spec = pl.BlockSpec((tq, dv), lambda b, h, i: (b * nq + i, h))
        k_specs = [pl.BlockSpec((CT, dk), lambda b, h, i: (cb + b, kh(h))),
                   pl.BlockSpec((S, dk), lambda b, h, i: (b, kh(h)))]
        v_specs = [pl.BlockSpec((CT, dv), lambda b, h, i: (cb + b, vbase + kh(h))),
                   pl.BlockSpec((S, dv), lambda b, h, i: (b, vbase + kh(h)))]
        seg_lens = (CT, S)
    nseg = len(seg_lens)
    kern = functools.partial(_flash_kernel, nseg=nseg, seg_lens=seg_lens, tq=tq, dv=dv, ck=512)
    return pl.pallas_call(
        kern,
        grid=(B, hq, nq),
        in_specs=[q_spec] + k_specs + v_specs,
        out_specs=o_spec,
        out_shape=jax.ShapeDtypeStruct((n_out, hq * dv), BF),
        scratch_shapes=[pltpu.VMEM((tq, sum(seg_lens)), F32)],
        compiler_params=_cp(("arbitrary", "arbitrary", "arbitrary")),
        name=name,
    )(qa, *([ka] * nseg), *([va] * nseg))


WOUT_TM = 512
COUNT_BITS = 20


def _route(logit, rb):
    T = logit.shape[1]
    gsz = N_EXP // N_GRP
    scores = _sigmoid(logit)
    sel = scores + rb
    ninf = jnp.float32(-jnp.inf)
    sel3 = sel.reshape(N_GRP, gsz, T)
    idx3 = lax.broadcasted_iota(I32, (N_GRP, gsz, T), 1)
    m1 = jnp.max(sel3, axis=1, keepdims=True)
    first = jnp.min(jnp.where(sel3 == m1, idx3, gsz), axis=1, keepdims=True)
    m2 = jnp.max(jnp.where(idx3 == first, ninf, sel3), axis=1, keepdims=True)
    gs = (m1 + m2).reshape(N_GRP, T)
    gi = lax.broadcasted_iota(I32, (N_GRP, T), 0)
    grank = jnp.zeros((N_GRP, T), I32)
    for j in range(N_GRP):
        row = gs[j:j + 1, :]
        grank = grank + ((row > gs) | ((row == gs) & (j < gi))).astype(I32)
    gkeep = (grank < TOPK_GRP).reshape(N_GRP, 1, T)
    selm = jnp.where(jnp.broadcast_to(gkeep, (N_GRP, gsz, T)), sel3, ninf).reshape(N_EXP, T)
    ei = lax.broadcasted_iota(I32, (N_EXP, T), 0)
    erank = jnp.zeros((N_EXP, T), I32)
    for j in range(N_EXP):
        row = selm[j:j + 1, :]
        erank = erank + ((row > selm) | ((row == selm) & (j < ei))).astype(I32)
    chosen = erank < TOP_K
    w = jnp.where(chosen, scores, 0.0)
    wd = w / jnp.sum(w, axis=0, keepdims=True) * ROUTED_SCALE
    return wd, chosen


def _count_to_int(x, nbits):
    out = jnp.zeros(x.shape, I32)
    for b in reversed(range(nbits)):
        ge = x >= float(1 << b)
        out = out + jnp.where(ge, 1 << b, 0)
        x = x - jnp.where(ge, float(1 << b), 0.0)
    return out


def _store_tile_rows(ref, packed, tm):
    for cblk in range(HALF // 128):
        ref[pl.ds(cblk, tm, stride=8), :] = packed[:, 128 * cblk:128 * (cblk + 1)]


def _load_tile_rows(ref, tm):
    los, his = [], []
    for cblk in range(HALF // 128):
        lo, hi = _unpack_halves(ref[pl.ds(cblk, tm, stride=8), :])
        los.append(lo.astype(BF))
        his.append(hi.astype(BF))
    return jnp.concatenate(los, axis=1), jnp.concatenate(his, axis=1)


def _wout_kernel(of_ref, ob_ref, gr_ref, mla_ref, gqa_ref, x_ref, w_ref, gout_ref, gpm_ref, gtm_ref,
                 gpf_ref, shf_ref, scf_ref, rwt_ref, rb_ref, ust_ref, lst_ref,
                 xn_ref, hfp_ref, r8_ref, e8_ref, w8_ref, cnt_ref, cnt_scr):
    @pl.when(pl.program_id(0) == 0)
    def _():
        cnt_scr[...] = jnp.zeros_like(cnt_scr)

    og = of_ref[...] + ob_ref[...]
    gr = gr_ref[...].astype(F32)
    parts = []
    for h in range(GLA_H):
        sv = slice(GLA_DV * h, GLA_DV * (h + 1))
        g = gr[:, sv]
        parts.append((_rms(og[:, sv], gout_ref[...]) * (g * _sigmoid(g))).astype(BF))
    gla = jnp.concatenate(parts, axis=1)
    n_g, n_m = GLA_H * GLA_DV, MLA_H * MLA_DV
    y = (_dot(gla, w_ref[0:n_g, :]) + _dot(mla_ref[...], w_ref[n_g:n_g + n_m, :])
         + _dot(gqa_ref[...], w_ref[n_g + n_m:, :]))
    xn = x_ref[...] + gtm_ref[0] * _rms(y, gpm_ref[...])
    xn_ref[...] = xn
    hf = _rms(xn, gpf_ref[...]) * (1.0 + scf_ref[0]) + shf_ref[0]
    _store_tile_rows(hfp_ref, _pack_halves(hf), WOUT_TM)
    hh, hl = _split2(hf)
    rh, rl = _split2(rwt_ref[...])
    logit = _dot_nt(rh, hh) + _dot_nt(rl, hh) + _dot_nt(rh, hl)
    wd, chosen = _route(logit, rb_ref[...])

    cb = jnp.where(chosen, 1.0, 0.0)
    before = _dot(cb.astype(BF), ust_ref[...])
    rank = cnt_scr[:, 0:1] + before
    slot = _dot(lst_ref[...], cb.astype(BF))
    eid = lax.broadcasted_iota(I32, chosen.shape, 0).astype(F32)
    r8, e8, w8 = [], [], []
    for k in range(TOP_K):
        pick = chosen & (slot == float(k))
        r8.append(jnp.sum(jnp.where(pick, rank, 0.0), axis=0, keepdims=True))
        e8.append(jnp.sum(jnp.where(pick, eid, 0.0), axis=0, keepdims=True))
        w8.append(jnp.sum(jnp.where(pick, wd, 0.0), axis=0, keepdims=True))
    r8_ref[...] = _count_to_int(jnp.concatenate(r8, axis=0), COUNT_BITS)
    e8_ref[...] = _count_to_int(jnp.concatenate(e8, axis=0), 6)
    w8_ref[...] = jnp.concatenate(w8, axis=0)
    cnt_scr[...] = cnt_scr[...] + jnp.sum(cb, axis=1, keepdims=True)
    cnt_ref[...] = _count_to_int(cnt_scr[...], COUNT_BITS)


def _wout(gla_o, proj, mla_o, gqa_o, xr, w, gout, gpm, gtm, gpf, shf, scf, rwt, rb, *, n_rows, tiles_per_batch, nb):
    tm = WOUT_TM
    mod_idx = lambda i: (jnp.minimum(i // tiles_per_batch, nb), 0, 0)
    row = lambda w_: pl.BlockSpec((tm, w_), lambda i: (i, 0))
    full = lambda a: pl.BlockSpec(a.shape, lambda i: (0,) * a.ndim)
    mspec = pl.BlockSpec((1, 1, D), mod_idx)
    it = np.arange(tm)
    ust = jnp.asarray(it[:, None] < it[None, :], BF)
    ie = np.arange(N_EXP)
    lst = jnp.asarray(ie[None, :] < ie[:, None], BF)
    k8 = pl.BlockSpec((TOP_K, tm), lambda i: (0, i))
    return pl.pallas_call(
        _wout_kernel,
        grid=(n_rows // tm,),
        in_specs=[row(512), row(512),
                  pl.BlockSpec((tm, 512), lambda i: (i, OFF_GR // 512)),
                  row(768), row(768), row(D), full(w), full(gout), full(gpm), mspec,
                  full(gpf), mspec, mspec, full(rwt), full(rb), full(ust), full(lst)],
        out_specs=[row(D), pl.BlockSpec((tm * 8, 128), lambda i: (i, 0)), k8, k8, k8,
                   pl.BlockSpec((N_EXP, 128), lambda i: (0, 0))],
        out_shape=[jax.ShapeDtypeStruct((n_rows, D), F32), jax.ShapeDtypeStruct((n_rows * 8, 128), PACKED),
                   jax.ShapeDtypeStruct((TOP_K, n_rows), I32), jax.ShapeDtypeStruct((TOP_K, n_rows), I32),
                   jax.ShapeDtypeStruct((TOP_K, n_rows), F32), jax.ShapeDtypeStruct((N_EXP, 128), I32)],
        scratch_shapes=[pltpu.VMEM((N_EXP, 128), F32)],
        compiler_params=_cp(("arbitrary",)),
        name="wout_route",
    )(gla_o[0], gla_o[1], proj, mla_o, gqa_o, xr, w, gout, gpm, gtm, gpf, shf, scf, rwt, rb, ust, lst)


MOE_TM = 512
DISP_TM = 256
COMB_TM = 128


def _tile_row(ref, r):
    return ref.at[pl.ds(pl.multiple_of(r * 8, 8), 8), :]


def _dispatch_kernel(cnt_ref, pst_ref, ntl_ref, nu_ref, pos_ref, hfp_ref, xs_ref, zero_scr, sem, zsem):
    i = pl.program_id(0)
    tm = DISP_TM
    tile_rows = MOE_TM * 8
    n_tiles = xs_ref.shape[0] // tile_rows

    def issue(t, carry):
        for k in range(TOP_K):
            p = pos_ref[0, 0, k * tm + t]
            pltpu.make_async_copy(_tile_row(hfp_ref, t), _tile_row(xs_ref, p), sem).start(priority=k % 2)
        return carry

    lax.fori_loop(0, tm, issue, 0, unroll=2)

    @pl.when(i == 0)
    def _():
        zero_scr[...] = _packed_zeros(zero_scr.shape)

        def pad_copies(e):
            cnt = cnt_ref[e]
            padlen = ntl_ref[e] * MOE_TM - cnt
            off = pst_ref[e] + cnt

            pieces = []
            bit = MOE_TM // 2
            while bit >= 1:
                take = padlen & bit
                dst = xs_ref.at[pl.ds(pl.multiple_of(off * 8, 8), bit * 8), :]
                pieces.append((take != 0, pltpu.make_async_copy(zero_scr.at[pl.ds(0, bit * 8), :], dst, zsem)))
                off = off + take
                bit //= 2
            return pieces

        def start_pads(e, carry):
            for go, cp in pad_copies(e):
                pl.when(go)(cp.start)
            return carry

        def wait_pads(e, carry):
            for go, cp in pad_copies(e):
                pl.when(go)(cp.wait)
            return carry

        lax.fori_loop(0, N_EXP, start_pads, 0)
        lax.fori_loop(0, N_EXP, wait_pads, 0)

        def ztile(t):
            dst = xs_ref.at[pl.ds(pl.multiple_of(t * tile_rows, tile_rows), tile_rows), :]
            return pltpu.make_async_copy(zero_scr, dst, zsem)

        lax.fori_loop(nu_ref[0], n_tiles, lambda t, cr: (ztile(t).start(), cr)[1], 0)
        lax.fori_loop(nu_ref[0], n_tiles, lambda t, cr: (ztile(t).wait(), cr)[1], 0)

    for k in range(TOP_K):
        pltpu.make_async_copy(hfp_ref, xs_ref.at[pl.ds(0, tm * 8), :], sem).wait()


def _dispatch(cnt, pst, ntl, n_used, pos_t, hfp, *, n_rows_sorted):
    n = hfp.shape[0] // 8
    tm = DISP_TM
    grid_spec = pltpu.PrefetchScalarGridSpec(
        num_scalar_prefetch=4,
        grid=(n // tm,),
        in_specs=[pl.BlockSpec((1, 1, tm * TOP_K), lambda i, *_: (i, 0, 0), memory_space=pltpu.SMEM),
                  pl.BlockSpec((tm * 8, 128), lambda i, *_: (i, 0))],
        out_specs=pl.BlockSpec(memory_space=pl.ANY),
        scratch_shapes=[pltpu.VMEM((MOE_TM * 8, 128), PACKED), pltpu.SemaphoreType.DMA, pltpu.SemaphoreType.DMA],
    )
    return pl.pallas_call(
        _dispatch_kernel,
        grid_spec=grid_spec,
        out_shape=jax.ShapeDtypeStruct((n_rows_sorted * 8, 128), PACKED),
        compiler_params=_cp(("arbitrary",)),
        name="moe_dispatch",
    )(cnt, pst, ntl, n_used, pos_t, hfp)


def _gmm_kernel(te_ref, nu_ref, nxt_ref, ord_ref, x_ref, w1_ref, w3_ref, w2_ref, y_ref,
                w1f, w3f, w2f, w1b, w3b, w2b, wsem, *, layer):
    i = pl.program_id(0)
    used = i < nu_ref[0]

    def weight_copies(e, slot):
        row = layer * N_EXP + e
        return [pltpu.make_async_copy(src.at[row], dst.at[slot], wsem.at[slot])
                for src, dst in ((w1_ref, w1f), (w3_ref, w3f), (w2_ref, w2f))]

    @pl.when(used)
    def _():
        e = te_ref[i]
        changed = jnp.logical_or(i == 0, te_ref[jnp.maximum(i - 1, 0)] != e)

        @pl.when(changed)
        def _():
            slot = ord_ref[e] % 2

            @pl.when(i == 0)
            def _():
                for cp in weight_copies(e, slot):
                    cp.start()

            for cp in weight_copies(e, slot):
                cp.wait()
            nxt = nxt_ref[e]

            @pl.when(nxt >= 0)
            def _():
                for cp in weight_copies(nxt, 1 - slot):
                    cp.start()

            w1b[...] = w1f[slot].astype(BF)
            w3b[...] = w3f[slot].astype(BF)
            w2b[...] = w2f[slot].astype(BF)

        xa, xb = _load_tile_rows(x_ref, MOE_TM)
        h1 = _dot(xa, w1b[0:HALF, :]) + _dot(xb, w1b[HALF:, :])
        h3 = _dot(xa, w3b[0:HALF, :]) + _dot(xb, w3b[HALF:, :])
        h = (h1 * _sigmoid(h1) * h3).astype(BF)
        _store_tile_rows(y_ref, _pack_halves(_dot(h, w2b[...])), MOE_TM)

    @pl.when(jnp.logical_not(used))
    def _():
        y_ref[...] = _packed_zeros(y_ref.shape)


def _gmm(tile_e, n_used, ntl, xs, w1, w3, w2, layer):
    tm = MOE_TM
    n_tiles = xs.shape[0] // (tm * 8)
    ie = jnp.arange(N_EXP, dtype=I32)
    has = ntl > 0
    later = (ie[None, :] > ie[:, None]) & has[None, :]
    nxt = jnp.min(jnp.where(later, ie[None, :], N_EXP), axis=1)
    nxt = jnp.where(nxt == N_EXP, -1, nxt).astype(I32)
    ordn = (jnp.sum(((ie[None, :] <= ie[:, None]) & has[None, :]).astype(I32), axis=1) - 1).astype(I32)
    any_spec = pl.BlockSpec(memory_space=pl.ANY)
    grid_spec = pltpu.PrefetchScalarGridSpec(
        num_scalar_prefetch=4,
        grid=(n_tiles,),
        in_specs=[pl.BlockSpec((tm * 8, 128), lambda i, te, nu, *_: (jnp.minimum(i, nu[0] - 1), 0)),
                  any_spec, any_spec, any_spec],
        out_specs=pl.BlockSpec((tm * 8, 128), lambda i, *_: (i, 0)),
        scratch_shapes=[pltpu.VMEM((2, D, D_EXP), F32), pltpu.VMEM((2, D, D_EXP), F32), pltpu.VMEM((2, D_EXP, D), F32),
                        pltpu.VMEM((D, D_EXP), BF), pltpu.VMEM((D, D_EXP), BF), pltpu.VMEM((D_EXP, D), BF),
                        pltpu.SemaphoreType.DMA((2,))],
    )
    return pl.pallas_call(
        functools.partial(_gmm_kernel, layer=layer),
        grid_spec=grid_spec,
        out_shape=jax.ShapeDtypeStruct(xs.shape, PACKED),
        compiler_params=_cp(("arbitrary",)),
        name="moe_experts",
    )(tile_e, n_used, nxt, ordn, xs, w1, w3, w2)


def _combine_kernel(pos_ref, posn_ref, w8_ref, hfp_ref, x_ref, s1_ref, s3_ref, s2_ref, gpost_ref, gtf_ref, ys_ref,
                    o_ref, ybuf_a, ybuf_b, sem_a, sem_b):
    tm = COMB_TM
    i = pl.program_id(0)
    last = pl.num_programs(0) - 1

    def gather(pref, buf, sem):
        for t in range(tm):
            for k in range(TOP_K):
                p = pref[0, 0, k * tm + t]
                pltpu.make_async_copy(_tile_row(ys_ref, p), buf.at[k, pl.ds(t * 8, 8), :], sem).start(priority=k % 2)

    def drain(buf, sem):
        for k in range(TOP_K):
            pltpu.make_async_copy(ys_ref.at[pl.ds(0, tm * 8), :], buf.at[k], sem).wait()

    @pl.when(i == 0)
    def _():
        gather(pos_ref, ybuf_a, sem_a)

    def body(cur, sem_c, nxt, sem_n):
        gather(posn_ref, nxt, sem_n)
        xa, xb = _load_tile_rows(hfp_ref, tm)
        h1 = _dot(xa, s1_ref[0:HALF, :]) + _dot(xb, s1_ref[HALF:, :])
        h3 = _dot(xa, s3_ref[0:HALF, :]) + _dot(xb, s3_ref[HALF:, :])
        sh = _dot((h1 * _sigmoid(h1) * h3).astype(BF), s2_ref[...])
        drain(cur, sem_c)
        w8 = w8_ref[...]
        wk = [w8[:, k:k + 1] for k in range(TOP_K)]
        los, his = [], []
        for cblk in range(HALF // 128):
            a_lo = sh[:, 128 * cblk:128 * (cblk + 1)]
            a_hi = sh[:, HALF + 128 * cblk:HALF + 128 * (cblk + 1)]
            for k in range(TOP_K):
                lo, hi = _unpack_halves(cur[k, pl.ds(cblk, tm, stride=8), :])
                a_lo = a_lo + wk[k] * lo
                a_hi = a_hi + wk[k] * hi
            los.append(a_lo)
            his.append(a_hi)
        yf = jnp.concatenate(los + his, axis=1)
        o_ref[...] = x_ref[...] + gtf_ref[0] * _rms(yf, gpost_ref[...])

        @pl.when(i == last)
        def _():
            drain(nxt, sem_n)

    @pl.when(i % 2 == 0)
    def _():
        body(ybuf_a, sem_a, ybuf_b, sem_b)

    @pl.when(i % 2 == 1)
    def _():
        body(ybuf_b, sem_b, ybuf_a, sem_a)


def _combine(pos_t, w8, hfp, xn, s1, s3, s2, gpost, gtf, ys, *, tiles_per_batch, nb):
    n = hfp.shape[0] // 8
    tm = COMB_TM
    nt = n // tm
    mod_idx = lambda i: (jnp.minimum(i // tiles_per_batch, nb), 0, 0)
    full = lambda a: pl.BlockSpec(a.shape, lambda i: (0,) * a.ndim)
    buf = pltpu.VMEM((TOP_K, tm * 8, 128), PACKED)
    return pl.pallas_call(
        _combine_kernel,
        grid=(nt,),
        in_specs=[pl.BlockSpec((1, 1, tm * TOP_K), lambda i: (i, 0, 0), memory_space=pltpu.SMEM),
                  pl.BlockSpec((1, 1, tm * TOP_K), lambda i: (jnp.minimum(i + 1, nt - 1), 0, 0),
                               memory_space=pltpu.SMEM),
                  pl.BlockSpec((tm, TOP_K), lambda i: (i, 0)),
                  pl.BlockSpec((tm * 8, 128), lambda i: (i, 0)),
                  pl.BlockSpec((tm, D), lambda i: (i, 0)),
                  full(s1), full(s3), full(s2), full(gpost),
                  pl.BlockSpec((1, 1, D), mod_idx),
                  pl.BlockSpec(memory_space=pl.ANY)],
        out_specs=pl.BlockSpec((tm, D), lambda i: (i, 0)),
        out_shape=jax.ShapeDtypeStruct((n, D), F32),
        scratch_shapes=[buf, buf, pltpu.SemaphoreType.DMA, pltpu.SemaphoreType.DMA],
        compiler_params=_cp(("arbitrary",)),
        name="moe_combine",
    )(pos_t, pos_t, w8, hfp, xn, s1, s3, s2, gpost, gtf, ys)


def _dispatch_tables(counts, r8, e8):
    n = r8.shape[1]
    ntl = (counts + MOE_TM - 1) // MOE_TM
    tend = jnp.cumsum(ntl)
    pst = (tend - ntl) * MOE_TM
    onehot = e8[None] == jnp.arange(N_EXP, dtype=I32)[:, None, None]
    pos = r8 + jnp.sum(jnp.where(onehot, pst[:, None, None], 0), axis=0)
    n_tiles = -(-(n * TOP_K) // MOE_TM) + N_EXP
    tile_e = jnp.sum((jnp.arange(n_tiles, dtype=I32)[:, None] >= tend[None, :]).astype(I32), axis=1)
    tile_e = jnp.minimum(tile_e, N_EXP - 1)
    return pst.astype(I32), ntl.astype(I32), pos.astype(I32), tile_e.astype(I32), tend[-1:].astype(I32), n_tiles


def _pos_blocks(pos, tm):
    k, n = pos.shape
    return pos.reshape(k, n // tm, tm).transpose(1, 0, 2).reshape(n // tm, 1, k * tm)


def _inproj_weight(w_in):
    offs = np.cumsum((0, 256, 256, 512, 512, 16, 16, 384, 512, 64, 768, 256, 256))
    p = [w_in[:, offs[i]:offs[i + 1]] for i in range(12)]
    gq, gk, gv, gr, gzf, gzb, cq, ckv, kr, aq, ak, av = p
    krz = jnp.concatenate([kr, gzf, gzb, jnp.zeros((D, 32), w_in.dtype)], axis=1)
    return jnp.concatenate([gv, gr, ckv, aq, cq, krz, gq, gk, ak, av], axis=1).astype(BF)


def _mla_weights(w_uq, w_ukv):
    wq = w_uq.reshape(MLA_QR, MLA_H, MLA_NOPE + MLA_ROPE)
    wq = jnp.concatenate([wq, jnp.zeros((MLA_QR, MLA_H, 256 - MLA_NOPE - MLA_ROPE), wq.dtype)], axis=2)
    wkv = w_ukv.reshape(MLA_KVR, MLA_H, MLA_NOPE + MLA_DV)
    wk = wkv[:, :, :MLA_NOPE].reshape(MLA_KVR, MLA_H * MLA_NOPE)
    wv = wkv[:, :, MLA_NOPE:].reshape(MLA_KVR, MLA_H * MLA_DV)
    return wq.reshape(MLA_QR, MLA_H * 256).astype(BF), wk.astype(BF), wv.astype(BF)


def _gla_gate_weights(wf, bf, wb, bb):
    z = jnp.zeros((128, GLA_H * GLA_DK), F32)
    wg = jnp.stack([z.at[64:80].set(wf), z.at[80:96].set(wb)]).astype(BF)
    bg = jnp.stack([bf, bb]).reshape(2, 1, GLA_H * GLA_DK)
    return wg, bg


def kernel(x, c, ctx, c_ctx, w_mod, b_mod, g_pre_mix, g_post_mix, g_pre_ffn, g_post_ffn, w_in, gla_wg2_f, gla_bg_f, gla_wg2_b, gla_bg_b, gla_g_out, mla_g_q, mla_w_uq, mla_g_kv, mla_w_ukv, gqa_g_q, gqa_g_k, w_out, router_w, router_b, exp_w1, exp_w3, exp_w2, sh_w1, sh_w3, sh_w2):
    B, S, _ = x.shape
    depth = w_mod.shape[0]
    assert ctx.shape[1] == CT and S % 256 == 0 and B < 8 and B * (S + CT) < (1 << COUNT_BITS)
    RL, RC = B * S, B * CT
    R = RL + RC

    xr = jnp.concatenate([x.reshape(RL, D), ctx.reshape(RC, D)], axis=0)
    cvecs = jnp.concatenate([c, c_ctx[None], jnp.zeros((8 - B - 1, D), F32)], axis=0)
    mods = _modulation(cvecs, w_mod, b_mod)
    tabs = _rope_tables(S)
    ii = np.arange(GLA_CHUNK)
    tri = jnp.asarray(np.stack([ii[:, None] >= ii[None, :], ii[:, None] <= ii[None, :]]), BF)

    in_tm = 1024 if (S % 1024 == 0 and RC % 1024 == 0) else 256
    row2 = lambda v: v.reshape(1, -1)
    ew1 = exp_w1.reshape(depth * N_EXP, D, D_EXP)
    ew3 = exp_w3.reshape(depth * N_EXP, D, D_EXP)
    ew2 = exp_w2.reshape(depth * N_EXP, D_EXP, D)

    for l in range(depth):
        want_ctx = l < depth - 1
        m = mods[l].reshape(8, 6, 1, D)
        sh_m, sc_m, gt_m, sh_f, sc_f, gt_f = [m[:, i] for i in range(6)]

        proj = _inproj(xr, row2(g_pre_mix[l]), sh_m, sc_m, _inproj_weight(w_in[l]),
                       tm=in_tm, tiles_per_batch=S // in_tm, nb=B)
        wuq, wk, wv = _mla_weights(mla_w_uq[l], mla_w_ukv[l])
        qm, km, vm, qg, kg = _prep(proj, row2(mla_g_q[l]), row2(mla_g_kv[l]), wuq, wk, wv,
                                   row2(gqa_g_q[l]), row2(gqa_g_k[l]), tabs, S=S, RL=RL)
        wg, bg = _gla_gate_weights(gla_wg2_f[l], gla_bg_f[l], gla_wg2_b[l], gla_bg_b[l])
        gla_o = _gla(proj, wg, bg, tri, B=B, S=S, RL=RL)

        att = functools.partial(_attention, B=B, S=S, RL=RL)
        mla = functools.partial(att, qm, km, vm, hq=MLA_H, group=1, dk=256, dv=MLA_DV, vbase=0)
        gqa = functools.partial(att, qg, kg, proj, hq=GQA_H, group=GQA_H // GQA_KV, dk=GQA_DH, dv=GQA_DH,
                                vbase=OFF_AV // GQA_DH)
        mla_o = mla(ctx_queries=False, name="mla_lat")
        gqa_o = gqa(ctx_queries=False, name="gqa_lat")
        if want_ctx:
            mla_o = jnp.concatenate([mla_o, mla(ctx_queries=True, name="mla_ctx")], axis=0)
            gqa_o = jnp.concatenate([gqa_o, gqa(ctx_queries=True, name="gqa_ctx")], axis=0)
        n_rows = R if want_ctx else RL

        xn, hfp, r8, e8, w8, cnt = _wout(gla_o, proj, mla_o, gqa_o, xr, w_out[l].astype(BF), row2(gla_g_out[l]),
                                         row2(g_post_mix[l]), gt_m, row2(g_pre_ffn[l]), sh_f, sc_f,
                                         router_w[l].T, router_b[l].reshape(N_EXP, 1),
                                         n_rows=n_rows, tiles_per_batch=S // WOUT_TM, nb=B)

        counts = cnt[:, 0]
        pst, ntl, pos, tile_e, n_used, n_tiles = _dispatch_tables(counts, r8, e8)
        xs = _dispatch(counts, pst, ntl, n_used, _pos_blocks(pos, DISP_TM), hfp, n_rows_sorted=n_tiles * MOE_TM)
        ys = _gmm(tile_e, n_used, ntl, xs, ew1, ew3, ew2, l)
        xo = _combine(_pos_blocks(pos, COMB_TM), w8.T, hfp, xn,
                      sh_w1[l].astype(BF), sh_w3[l].astype(BF), sh_w2[l].astype(BF),
                      row2(g_post_ffn[l]), gt_f, ys, tiles_per_batch=S // COMB_TM, nb=B)
        xr = xo
    return xr[:RL].reshape(B, S, D)
```
